```python
import math
import jax
import jax.numpy as jnp
from jax import lax
import numpy as np

D_MODEL = 1024
BATCH = 32
SEQ = 2048
DEPTH = 1

F32 = jnp.float32
MEM_LEN = 256
DIFF_HEADS = 8
DIFF_HEAD_DIM = 64
DIFF_V_DIM = 2 * DIFF_HEAD_DIM
ROPE_THETA = 500000.0
ROT_DIM = DIFF_HEAD_DIM // 4
Q_BLOCK = 128
GDN_HEADS = 8
GDN_HEAD_DIM = 128
CONV_WIDTH = 4
CHUNK = 64
MEM_HEADS = 4
MEM_HEAD_DIM = D_MODEL // MEM_HEADS
N_EXPERTS = 32
TOP_K = 4
EXPERT_FF = D_MODEL
SWIGLU_LIMIT = 7.0
SWIGLU_ALPHA = 1.702
EXPERT_BLOCK = 128
DEEPNORM_ALPHA = (2 * DEPTH) ** 0.25
DEEPNORM_BETA = (8 * DEPTH) ** -0.25
DIFF_QK_W = DIFF_HEADS * 2 * DIFF_HEAD_DIM
DIFF_V_W = DIFF_HEADS * DIFF_V_DIM
GDN_W = GDN_HEADS * GDN_HEAD_DIM
IN_SPLITS = (DIFF_QK_W, DIFF_QK_W, DIFF_V_W, GDN_W, GDN_W, GDN_W, GDN_W,
             GDN_HEADS, GDN_HEADS, D_MODEL, D_MODEL)
IN_PROJ_W = 3 * DIFF_QK_W + 4 * GDN_W + 2 * GDN_HEADS + 2 * D_MODEL

kernel_name = "hybrid_diffattn_gdn_moe_deepnorm"


def split_cols(t, sizes):
    outs, start = [], 0
    for s in sizes:
        outs.append(t[..., start:start + s])
        start += s
    return outs


def layer_norm(x, g, b, eps=1e-5):
    xf = x.astype(F32)
    mu = jnp.mean(xf, -1, keepdims=True)
    var = jnp.mean(jnp.square(xf - mu), -1, keepdims=True)
    return ((xf - mu) * lax.rsqrt(var + eps) * g.astype(F32) + b.astype(F32)).astype(x.dtype)


def rms_norm(x, g, eps=1e-6):
    xf = x.astype(F32)
    return (xf * lax.rsqrt(jnp.mean(xf * xf, -1, keepdims=True) + eps) * g.astype(F32)).astype(x.dtype)


def l2_normalize(t, eps=1e-6):
    return t * lax.rsqrt(jnp.sum(t * t, -1, keepdims=True) + eps)


def rope_tables(positions):
    inv_freq = jnp.power(ROPE_THETA, -jnp.arange(0, ROT_DIM, 2, dtype=F32) / ROT_DIM)
    ang = positions.astype(F32)[..., None] * inv_freq
    return jnp.cos(ang), jnp.sin(ang)


def apply_partial_rope(t, cos, sin):
    half = ROT_DIM // 2
    c = cos[:, :, None, None, :]
    s = sin[:, :, None, None, :]
    t1 = t[..., :half].astype(F32)
    t2 = t[..., half:ROT_DIM].astype(F32)
    rot = jnp.concatenate([t1 * c - t2 * s, t2 * c + t1 * s], -1).astype(t.dtype)
    return jnp.concatenate([rot, t[..., ROT_DIM:]], -1)


def differential_attention(q, k, v, cos, sin, lam_q1, lam_k1, lam_q2, lam_k2, subln_g, lambda_init):
    B, S, _ = q.shape
    q = apply_partial_rope(q.reshape(B, S, DIFF_HEADS, 2, DIFF_HEAD_DIM), cos, sin)
    k = apply_partial_rope(k.reshape(B, S, DIFF_HEADS, 2, DIFF_HEAD_DIM), cos, sin)
    q1, q2 = q[..., 0, :].transpose(0, 2, 1, 3), q[..., 1, :].transpose(0, 2, 1, 3)
    k1, k2 = k[..., 0, :].transpose(0, 2, 1, 3), k[..., 1, :].transpose(0, 2, 1, 3)
    v = v.reshape(B, S, DIFF_HEADS, DIFF_V_DIM).transpose(0, 2, 1, 3)
    lam = (jnp.exp(jnp.sum(lam_q1.astype(F32) * lam_k1.astype(F32)))
           - jnp.exp(jnp.sum(lam_q2.astype(F32) * lam_k2.astype(F32))) + lambda_init)
    scale = DIFF_HEAD_DIM ** -0.5
    nb = S // Q_BLOCK

    def to_blocks(t):
        return t.reshape(B, DIFF_HEADS, nb, Q_BLOCK, t.shape[-1]).transpose(2, 0, 1, 3, 4)

    k_pos = jnp.arange(S)

    def one_block(args):
        i, qb1, qb2 = args
        q_pos = i * Q_BLOCK + jnp.arange(Q_BLOCK)
        causal = k_pos[None, :] <= q_pos[:, None]

        def probs(qb, kk):
            s = jnp.einsum('bhqd,bhkd->bhqk', qb, kk).astype(F32) * scale
            return jax.nn.softmax(jnp.where(causal, s, -jnp.inf), axis=-1)

        p = probs(qb1, k1) - lam * probs(qb2, k2)
        return jnp.einsum('bhqk,bhkd->bhqd', p.astype(v.dtype), v)

    o = lax.map(one_block, (jnp.arange(nb), to_blocks(q1), to_blocks(q2)))
    o = o.transpose(1, 2, 0, 3, 4).reshape(B, DIFF_HEADS, S, DIFF_V_DIM)
    o = rms_norm(o, subln_g) * (1.0 - lambda_init)
    return o.transpose(0, 2, 1, 3).reshape(B, S, DIFF_V_W)


def chunked_gated_delta_rule(q, k, v, g, beta):
    B, S, H, dk = q.shape
    dv = v.shape[-1]
    N = S // CHUNK

    def to_chunks(t):
        t = t.reshape((B, N, CHUNK, H) + t.shape[3:])
        return jnp.moveaxis(jnp.moveaxis(t, 1, 0), 3, 2)

    q, k, v, g, beta = (to_chunks(t) for t in (q, k, v, g, beta))
    G = jnp.cumsum(g, axis=-1)
    idx = jnp.arange(CHUNK)
    incl = idx[:, None] >= idx[None, :]
    strict = idx[:, None] > idx[None, :]
    gap = G[..., :, None] - G[..., None, :]
    decay = jnp.where(incl, jnp.exp(jnp.where(incl, gap, 0.0)), 0.0)
    A = jnp.where(strict, jnp.einsum('nbhid,nbhjd->nbhij', k, k) * decay, 0.0) * beta[..., :, None]
    eG = jnp.exp(G)
    rhs = jnp.concatenate([v * beta[..., None], k * (beta * eG)[..., None]], -1)
    sol = lax.linalg.triangular_solve(A + jnp.eye(CHUNK, dtype=A.dtype), rhs,
                                      left_side=True, lower=True, unit_diagonal=True)
    u, w = sol[..., :dv], sol[..., dv:]
    qk = jnp.einsum('nbhid,nbhjd->nbhij', q, k) * decay
    q_dec = q * eG[..., None]
    k_dec = k * jnp.exp(G[..., -1:] - G)[..., None]
    g_last = eG[..., -1]

    def step(state, inp):
        u_n, w_n, qd_n, kd_n, qk_n, gl_n = inp
        v_new = u_n - jnp.einsum('bhck,bhkv->bhcv', w_n, state)
        o_n = jnp.einsum('bhck,bhkv->bhcv', qd_n, state) + jnp.einsum('bhij,bhjv->bhiv', qk_n, v_new)
        state = state * gl_n[..., None, None] + jnp.einsum('bhck,bhcv->bhkv', kd_n, v_new)
        return state, o_n

    state0 = jnp.zeros((B, H, dk, dv), F32)
    _, o = lax.scan(step, state0, (u, w, q_dec, k_dec, qk, g_last))
    return jnp.moveaxis(jnp.moveaxis(o, 2, 3), 0, 1).reshape(B, S, H, dv)


def gated_deltanet(q, k, v, z, b, a, conv_w, A_log, dt_bias, norm_g):
    B, S, _ = q.shape
    qkv = jnp.concatenate([q, k, v], -1)
    C = qkv.shape[-1]
    qkv = lax.conv_general_dilated(qkv, conv_w.astype(qkv.dtype)[:, None, :], (1,),
                                   [(CONV_WIDTH - 1, 0)],
                                   dimension_numbers=('NWC', 'WIO', 'NWC'),
                                   feature_group_count=C)
    qkv = jax.nn.silu(qkv)
    q, k, v = split_cols(qkv, (GDN_W, GDN_W, GDN_W))
    hd = (B, S, GDN_HEADS, GDN_HEAD_DIM)
    q = l2_normalize(q.reshape(hd).astype(F32)) * (GDN_HEAD_DIM ** -0.5)
    k = l2_normalize(k.reshape(hd).astype(F32))
    v = v.reshape(hd).astype(F32)
    beta = jax.nn.sigmoid(b.astype(F32))
    g = -jnp.exp(A_log.astype(F32)) * jax.nn.softplus(a.astype(F32) + dt_bias.astype(F32))
    o = chunked_gated_delta_rule(q, k, v, g, beta)
    o = rms_norm(o, norm_g) * jax.nn.silu(z.reshape(hd).astype(F32))
    return o.reshape(B, S, GDN_W).astype(z.dtype)


def memory_cross_attention(x, mem, w_q, w_k, w_v, w_o):
    B, S, _ = x.shape
    M = mem.shape[1]
    q = (x @ w_q).reshape(B, S, MEM_HEADS, MEM_HEAD_DIM)
    k = (mem @ w_k).reshape(B, M, MEM_HEADS, MEM_HEAD_DIM)
    v = (mem @ w_v).reshape(B, M, MEM_HEADS, MEM_HEAD_DIM)
    s = jnp.einsum('bshd,bmhd->bhsm', q, k).astype(F32) * (MEM_HEAD_DIM ** -0.5)
    p = jax.nn.softmax(s, axis=-1)
    o = jnp.einsum('bhsm,bmhd->bshd', p.astype(v.dtype), v).reshape(B, S, D_MODEL)
    return o @ w_o


def moe_ffn(x, w_router, b_router, w1, b1, w2, b2):
    B, S, D = x.shape
    T = B * S
    TK = T * TOP_K
    xf = x.reshape(T, D)
    logits = (xf @ w_router).astype(F32) + b_router.astype(F32)
    top_val, top_idx = lax.top_k(logits, TOP_K)
    gates = jax.nn.softmax(top_val, axis=-1)
    flat_e = top_idx.reshape(TK)
    order = jnp.argsort(flat_e)
    sorted_e = flat_e[order]
    counts = jnp.bincount(flat_e, length=N_EXPERTS)
    starts = jnp.cumsum(counts) - counts
    padded = (counts + EXPERT_BLOCK - 1) // EXPERT_BLOCK * EXPERT_BLOCK
    pad_ends = jnp.cumsum(padded)
    pad_starts = pad_ends - padded
    dest = pad_starts[sorted_e] + jnp.arange(TK) - starts[sorted_e]
    n_blocks = -(-TK // EXPERT_BLOCK) + N_EXPERTS
    R = n_blocks * EXPERT_BLOCK
    row_tok = jnp.full((R,), T, jnp.int32).at[dest].set((order // TOP_K).astype(jnp.int32))
    row_gate = jnp.zeros((R,), F32).at[dest].set(gates.reshape(TK)[order])
    block_e = jnp.minimum(jnp.searchsorted(pad_ends, jnp.arange(n_blocks) * EXPERT_BLOCK, side='right'),
                          N_EXPERTS - 1)
    x_pad = jnp.concatenate([xf, jnp.zeros((1, D), xf.dtype)], 0)

    def one_block(acc, inp):
        tok, gate, e = inp
        h = x_pad[tok] @ w1[e] + b1[e]
        h_gate, h_up = h[:, :EXPERT_FF], h[:, EXPERT_FF:]
        h_gate = jnp.minimum(h_gate, SWIGLU_LIMIT)
        h_up = jnp.clip(h_up, -SWIGLU_LIMIT, SWIGLU_LIMIT)
        act = h_gate * jax.nn.sigmoid(SWIGLU_ALPHA * h_gate) * (h_up + 1.0)
        y = (act @ w2[e] + b2[e]).astype(F32)
        return acc.at[tok].add(y * gate[:, None]), None

    acc0 = jnp.zeros((T + 1, D), F32)
    acc, _ = lax.scan(one_block, acc0, (row_tok.reshape(n_blocks, EXPERT_BLOCK),
                                         row_gate.reshape(n_blocks, EXPERT_BLOCK), block_e))
    return acc[:T].reshape(B, S, D).astype(x.dtype)


def setup_inputs(seed: int = 0) -> dict:
    key = jax.random.key(seed)
    ks = jax.random.split(key, 40)
    L, D = DEPTH, D_MODEL

    def nrm(k, shape, scale):
        return jax.random.normal(k, shape, F32) * scale

    dt = jnp.exp(jax.random.uniform(ks[11], (L, GDN_HEADS), F32, math.log(1e-3), math.log(1e-1)))
    return {
        'x': nrm(ks[0], (BATCH, SEQ, D), 1.0),
        'mem': nrm(ks[1], (BATCH, MEM_LEN, D), 1.0),
        'positions': jnp.broadcast_to(jnp.arange(SEQ, dtype=jnp.int32)[None, :], (BATCH, SEQ)),
        'w_in': nrm(ks[2], (L, D, IN_PROJ_W), D ** -0.5),
        'diff_lambda_q1': nrm(ks[3], (L, DIFF_HEAD_DIM), 0.1),
        'diff_lambda_k1': nrm(ks[4], (L, DIFF_HEAD_DIM), 0.1),
        'diff_lambda_q2': nrm(ks[5], (L, DIFF_HEAD_DIM), 0.1),
        'diff_lambda_k2': nrm(ks[6], (L, DIFF_HEAD_DIM), 0.1),
        'diff_subln_g': 1.0 + nrm(ks[7], (L, DIFF_V_DIM), 0.02),
        'w_diff_o': nrm(ks[8], (L, DIFF_V_W, D), DIFF_V_W ** -0.5),
        'gdn_conv_w': nrm(ks[9], (L, CONV_WIDTH, 3 * GDN_W), CONV_WIDTH ** -0.5),
        'gdn_A_log': jnp.log(jax.random.uniform(ks[10], (L, GDN_HEADS), F32, 1.0, 16.0)),
        'gdn_dt_bias': dt + jnp.log(-jnp.expm1(-dt)),
        'gdn_norm_g': 1.0 + nrm(ks[12], (L, GDN_HEAD_DIM), 0.02),
        'w_gdn_o': nrm(ks[13], (L, GDN_W, D), GDN_W ** -0.5),
        'w_mix_o': nrm(ks[14], (L, D, D), D ** -0.5 * DEEPNORM_BETA),
        'ln1_g': 1.0 + nrm(ks[15], (L, D), 0.02),
        'ln1_b': nrm(ks[16], (L, D), 0.01),
        'w_cq': nrm(ks[17], (L, D, D), D ** -0.5),
        'w_ck': nrm(ks[18], (L, D, D), D ** -0.5),
        'w_cv': nrm(ks[19], (L, D, D), D ** -0.5),
        'w_co': nrm(ks[20], (L, D, D), D ** -0.5 * DEEPNORM_BETA),
        'ln2_g': 1.0 + nrm(ks[21], (L, D), 0.02),
        'ln2_b': nrm(ks[22], (L, D), 0.01),
        'w_router': nrm(ks[23], (L, D, N_EXPERTS), D ** -0.5),
        'b_router': nrm(ks[24], (L, N_EXPERTS), 0.01),
        'w_exp_in': nrm(ks[25], (L, N_EXPERTS, D, 2 * EXPERT_FF), D ** -0.5),
        'b_exp_in': nrm(ks[26], (L, N_EXPERTS, 2 * EXPERT_FF), 0.01),
        'w_exp_out': nrm(ks[27], (L, N_EXPERTS, EXPERT_FF, D), EXPERT_FF ** -0.5 * DEEPNORM_BETA),
        'b_exp_out': nrm(ks[28], (L, N_EXPERTS, D), 0.01),
        'ln3_g': 1.0 + nrm(ks[29], (L, D), 0.02),
        'ln3_b': nrm(ks[30], (L, D), 0.01),
    }


def reference(x, mem, positions, w_in, diff_lambda_q1, diff_lambda_k1, diff_lambda_q2,
              diff_lambda_k2, diff_subln_g, w_diff_o, gdn_conv_w, gdn_A_log, gdn_dt_bias,
              gdn_norm_g, w_gdn_o, w_mix_o, ln1_g, ln1_b, w_cq, w_ck, w_cv, w_co, ln2_g, ln2_b,
              w_router, b_router, w_exp_in, b_exp_in, w_exp_out, b_exp_out, ln3_g, ln3_b):
    cos, sin = rope_tables(positions)
    for l in range(DEPTH):
        lambda_init = 0.8 - 0.6 * math.exp(-0.3 * l)
        proj = x @ w_in[l]
        (dq, dk, dv, gq, gk, gv, gz, gb, ga, gate_a, gate_b) = split_cols(proj, IN_SPLITS)
        y_diff = differential_attention(dq, dk, dv, cos, sin, diff_lambda_q1[l], diff_lambda_k1[l],
                                        diff_lambda_q2[l], diff_lambda_k2[l], diff_subln_g[l],
                                        lambda_init) @ w_diff_o[l]
        y_gdn = gated_deltanet(gq, gk, gv, gz, gb, ga, gdn_conv_w[l], gdn_A_log[l], gdn_dt_bias[l],
                               gdn_norm_g[l]) @ w_gdn_o[l]
        mixed = (jax.nn.sigmoid(gate_a) * y_diff + jax.nn.sigmoid(gate_b) * y_gdn) @ w_mix_o[l]
        x = layer_norm(DEEPNORM_ALPHA * x + mixed, ln1_g[l], ln1_b[l])
        y_mem = memory_cross_attention(x, mem, w_cq[l], w_ck[l], w_cv[l], w_co[l])
        x = layer_norm(DEEPNORM_ALPHA * x + y_mem, ln2_g[l], ln2_b[l])
        y_moe = moe_ffn(x, w_router[l], b_router[l], w_exp_in[l], b_exp_in[l], w_exp_out[l], b_exp_out[l])
        x = layer_norm(DEEPNORM_ALPHA * x + y_moe, ln3_g[l], ln3_b[l])
    return x
```

```python
import functools
import math

import jax
import jax.numpy as jnp
from jax import lax
from jax.experimental import pallas as pl
from jax.experimental.pallas import tpu as pltpu

F32 = jnp.float32
BF16 = jnp.bfloat16
I32 = jnp.int32

D_MODEL = 1024
DIFF_HEADS = 8
DIFF_HEAD_DIM = 64
ROPE_THETA = 500000.0
ROT_DIM = DIFF_HEAD_DIM // 4
ROT_HALF = ROT_DIM // 2
GDN_HEADS = 8
GDN_HEAD_DIM = 128
CONV_WIDTH = 4
GDN_CHUNK = 64
MEM_HEADS = 4
MEM_HEAD_DIM = D_MODEL // MEM_HEADS
N_EXPERTS = 32
TOP_K = 4
SWIGLU_LIMIT = 7.0
SWIGLU_ALPHA = 1.702
LANES = 128
VMEM_LIMIT = 48 * 1024 * 1024

TOKEN_TILE = 512
ATTN_Q_TILE = 256
GDN_CHUNKS_PER_STEP = 2
ROUTE_TILE = 256
EXPERT_TILE = 512
HIGHEST = lax.Precision.HIGHEST


def _params(*sem):
    return pltpu.CompilerParams(dimension_semantics=sem, vmem_limit_bytes=VMEM_LIMIT)


def _dot(a, b):
    return jnp.dot(a, b, preferred_element_type=F32)


def _dot_nt(a, b):
    return lax.dot_general(a, b, (((1,), (1,)), ((), ())), preferred_element_type=F32)


def _dot_tn(a, b):
    return lax.dot_general(a, b, (((0,), (0,)), ((), ())), preferred_element_type=F32)


def _sigmoid(x):
    return 1.0 / (1.0 + jnp.exp(-x))


def _layer_norm(r, g, b):
    mu = jnp.mean(r, -1, keepdims=True)
    d = r - mu
    var = jnp.mean(d * d, -1, keepdims=True)
    return d * lax.rsqrt(var + 1e-5) * g + b


def _rope_kernel(pos_ref, inv_ref, cos_ref, sin_ref):
    ang = pos_ref[...] * inv_ref[...]
    cos_ref[...] = jnp.cos(ang)
    sin_ref[...] = jnp.sin(ang)


def _rope_tables(positions):
    T = positions.size
    inv_freq = jnp.power(ROPE_THETA, -jnp.arange(0, ROT_DIM, 2, dtype=F32) / ROT_DIM)
    per_row = LANES // ROT_HALF
    rows = T // per_row
    pos = jnp.repeat(positions.reshape(T).astype(F32), ROT_HALF).reshape(rows, LANES)
    inv = jnp.tile(inv_freq, per_row).reshape(1, LANES)
    rb = min(rows, 512)
    cos, sin = pl.pallas_call(
        _rope_kernel,
        grid=(rows // rb,),
        in_specs=[pl.BlockSpec((rb, LANES), lambda i: (i, 0)),
                  pl.BlockSpec((1, LANES), lambda i: (0, 0))],
        out_specs=[pl.BlockSpec((rb, LANES), lambda i: (i, 0))] * 2,
        out_shape=[jax.ShapeDtypeStruct((rows, LANES), F32)] * 2,
        compiler_params=_params("arbitrary"),
    )(pos, inv)
    cos = cos.reshape(T, ROT_HALF)
    sin = sin.reshape(T, ROT_HALF)
    rest = DIFF_HEAD_DIM - ROT_DIM
    one = jnp.ones((T, rest), F32)
    zero = jnp.zeros((T, rest), F32)
    z8 = jnp.zeros((T, ROT_HALF), F32)
    cos_t = jnp.concatenate([cos, cos, one] * 2, -1)
    sin_a = jnp.concatenate([-sin, z8, zero] * 2, -1)
    sin_b = jnp.concatenate([z8, sin, zero] * 2, -1)
    return cos_t, sin_a, sin_b


def _proj_diff_kernel(x_ref, w_ref, cos_ref, sa_ref, sb_ref, q_ref, k_ref, v_ref):
    xb = x_ref[...].astype(BF16)
    cos_t, sin_a, sin_b = cos_ref[...], sa_ref[...], sb_ref[...]
    qk_w = DIFF_HEADS * 2 * DIFF_HEAD_DIM
    for part, o_ref in enumerate((q_ref, k_ref)):
        acc = _dot(xb, w_ref[:, part * qk_w:(part + 1) * qk_w])
        for h in range(DIFF_HEADS):
            t = acc[:, h * LANES:(h + 1) * LANES]
            r = (t * cos_t + pltpu.roll(t, LANES - ROT_HALF, 1) * sin_a
                 + pltpu.roll(t, ROT_HALF, 1) * sin_b)
            if part == 0:
                r = r * (DIFF_HEAD_DIM ** -0.5)
            o_ref[:, h * LANES:(h + 1) * LANES] = r.astype(BF16)
    v_ref[...] = _dot(xb, w_ref[:, 2 * qk_w:]).astype(BF16)


def _proj_plain_kernel(x_ref, w_ref, o_ref):
    o_ref[...] = _dot(x_ref[...].astype(BF16), w_ref[...]).astype(BF16)


def _proj_gate_kernel(x_ref, w_ref, sz_ref, ga_ref, gb_ref):
    xb = x_ref[...].astype(BF16)
    z = _dot(xb, w_ref[:, :D_MODEL])
    sz_ref[...] = (z * _sigmoid(z)).astype(BF16)
    ga_ref[...] = _sigmoid(_dot(xb, w_ref[:, D_MODEL:2 * D_MODEL])).astype(BF16)
    gb_ref[...] = _sigmoid(_dot(xb, w_ref[:, 2 * D_MODEL:])).astype(BF16)


def _decay_kernel(x_ref, wt_ref, alog_ref, dtb_ref, beta_ref, g_ref):
    ba = lax.dot_general(wt_ref[...], x_ref[...], (((1,), (1,)), ((), ())),
                         precision=HIGHEST, preferred_element_type=F32)
    b, a = ba[:GDN_HEADS], ba[GDN_HEADS:]
    beta_ref[...] = _sigmoid(b)
    s = a + dtb_ref[...]
    softplus = jnp.maximum(s, 0.0) + jnp.log1p(jnp.exp(-jnp.abs(s)))
    g = -jnp.exp(alog_ref[...]) * softplus
    tm = g.shape[1]
    r = lax.broadcasted_iota(I32, (tm, tm), 0)
    c = lax.broadcasted_iota(I32, (tm, tm), 1)
    upper = ((r <= c) & (r // GDN_CHUNK == c // GDN_CHUNK)).astype(F32)
    g_ref[...] = jnp.dot(g, upper, precision=HIGHEST, preferred_element_type=F32)


def _token_spec(tm, width):
    return pl.BlockSpec((tm, width), lambda i: (i, 0))


def _const_spec(shape):
    return pl.BlockSpec(shape, lambda *_: (0,) * len(shape))


def _input_projections(xf, w_in, rope, a_log, dt_bias):
    T = xf.shape[0]
    tm = min(TOKEN_TILE, T)
    grid = (T // tm,)
    d = D_MODEL
    w_diff = w_in[:, :3 * d].astype(BF16)
    w_gdn = w_in[:, 3 * d:6 * d].astype(BF16)
    w_z = w_in[:, 6 * d:7 * d]
    w_ba_t = w_in[:, 7 * d:7 * d + 2 * GDN_HEADS].T
    w_gate = jnp.concatenate([w_z, w_in[:, 7 * d + 2 * GDN_HEADS:]], 1).astype(BF16)
    x_spec = _token_spec(tm, d)
    act = jax.ShapeDtypeStruct((T, d), BF16)

    dq, dk, dv = pl.pallas_call(
        _proj_diff_kernel, grid=grid,
        in_specs=[x_spec, _const_spec((d, 3 * d))] + [_token_spec(tm, LANES)] * 3,
        out_specs=[_token_spec(tm, d)] * 3, out_shape=[act] * 3,
        compiler_params=_params("arbitrary"),
    )(xf, w_diff, *rope)

    gdn_raw = pl.pallas_call(
        _proj_plain_kernel, grid=grid,
        in_specs=[x_spec, _const_spec((d, 3 * d))],
        out_specs=_token_spec(tm, 3 * d),
        out_shape=jax.ShapeDtypeStruct((T, 3 * d), BF16),
        compiler_params=_params("arbitrary"),
    )(xf, w_gdn)

    sz, sga, sgb = pl.pallas_call(
        _proj_gate_kernel, grid=grid,
        in_specs=[x_spec, _const_spec((d, 3 * d))],
        out_specs=[_token_spec(tm, d)] * 3, out_shape=[act] * 3,
        compiler_params=_params("arbitrary"),
    )(xf, w_gate)

    td = min(256, T)
    lane_spec = pl.BlockSpec((GDN_HEADS, td), lambda i: (0, i))
    beta_t, g_t = pl.pallas_call(
        _decay_kernel, grid=(T // td,),
        in_specs=[_token_spec(td, d), _const_spec((2 * GDN_HEADS, d)),
                  _const_spec((GDN_HEADS, 1)), _const_spec((GDN_HEADS, 1))],
        out_specs=[lane_spec] * 2,
        out_shape=[jax.ShapeDtypeStruct((GDN_HEADS, T), F32)] * 2,
        compiler_params=_params("arbitrary"),
    )(xf, w_ba_t, a_log.reshape(GDN_HEADS, 1), dt_bias.reshape(GDN_HEADS, 1))
    return dq, dk, dv, gdn_raw, sz, sga, sgb, beta_t, g_t


def _diff_attn_kernel(lam_ref, g_ref, q_ref, k_ref, v_ref, o_ref, *, lambda_init):
    tq = q_ref.shape[0]
    qi = pl.program_id(2)
    q = q_ref[...]
    lane = lax.broadcasted_iota(I32, q.shape, 1)
    zero = jnp.zeros_like(q)
    q2 = jnp.concatenate([jnp.where(lane < DIFF_HEAD_DIM, q, zero),
                          jnp.where(lane >= DIFF_HEAD_DIM, q, zero)], 0)

    def block(j, carry, masked):
        m, l, acc = carry
        kb = k_ref[pl.ds(pl.multiple_of(j * tq, tq), tq), :]
        vb = v_ref[pl.ds(pl.multiple_of(j * tq, tq), tq), :]
        s = _dot_nt(q2, kb)
        if masked:
            row = lax.broadcasted_iota(I32, (2 * tq, tq), 0)
            row = jnp.where(row >= tq, row - tq, row)
            col = lax.broadcasted_iota(I32, (2 * tq, tq), 1)
            s = jnp.where(col <= row, s, -jnp.inf)
        m_new = jnp.maximum(m, jnp.max(s, -1, keepdims=True))
        alpha = jnp.exp(m - m_new)
        p = jnp.exp(s - m_new)
        l = alpha * l + jnp.sum(p, -1, keepdims=True)
        acc = alpha * acc + _dot(p.astype(BF16), vb)
        return m_new, l, acc

    init = (jnp.full((2 * tq, 1), -jnp.inf, F32), jnp.zeros((2 * tq, 1), F32),
            jnp.zeros((2 * tq, LANES), F32))
    carry = lax.fori_loop(0, qi, lambda j, c: block(j, c, False), init)
    m, l, acc = block(qi, carry, True)
    o = acc / l
    lv = lam_ref[...]
    lam = (jnp.exp(jnp.sum(lv[0:1] * lv[1:2], -1, keepdims=True))
           - jnp.exp(jnp.sum(lv[2:3] * lv[3:4], -1, keepdims=True)) + lambda_init)
    o = o[:tq] - lam * o[tq:]
    o = o * lax.rsqrt(jnp.mean(o * o, -1, keepdims=True) + 1e-6) * g_ref[...]
    o_ref[...] = (o * (1.0 - lambda_init)).astype(BF16)


def _diff_attention(dq, dk, dv, lam_vecs, subln_g, B, S, lambda_init):
    tq = min(ATTN_Q_TILE, S)
    d = D_MODEL
    q3, k3, v3 = (t.reshape(B, S, d) for t in (dq, dk, dv))
    out = pl.pallas_call(
        functools.partial(_diff_attn_kernel, lambda_init=lambda_init),
        grid=(B, DIFF_HEADS, S // tq),
        in_specs=[_const_spec((4, DIFF_HEAD_DIM)), _const_spec((1, LANES)),
                  pl.BlockSpec((None, tq, LANES), lambda b, h, i: (b, i, h)),
                  pl.BlockSpec((None, S, LANES), lambda b, h, i: (b, 0, h)),
                  pl.BlockSpec((None, S, LANES), lambda b, h, i: (b, 0, h))],
        out_specs=pl.BlockSpec((None, tq, LANES), lambda b, h, i: (b, i, h)),
        out_shape=jax.ShapeDtypeStruct((B, S, d), BF16),
        compiler_params=_params("arbitrary", "arbitrary", "arbitrary"),
    )(lam_vecs, subln_g.reshape(1, LANES), q3, k3, v3)
    return out.reshape(B * S, d)


def _gdn_prep_kernel(x_ref, cw_ref, o_ref, *, kind):
    x = x_ref[...].astype(F32)
    row = lax.broadcasted_iota(I32, x.shape, 0)
    y = x * cw_ref[CONV_WIDTH - 1:CONV_WIDTH, :]
    for j in range(1, CONV_WIDTH):
        shifted = jnp.where(row >= j, pltpu.roll(x, j, 0), 0.0)
        y = y + shifted * cw_ref[CONV_WIDTH - 1 - j:CONV_WIDTH - j, :]
    y = y * _sigmoid(y)
    if kind == "v":
        o_ref[...] = y.astype(BF16)
        return
    for h in range(x.shape[1] // GDN_HEAD_DIM):
        yh = y[:, h * GDN_HEAD_DIM:(h + 1) * GDN_HEAD_DIM]
        yh = yh * lax.rsqrt(jnp.sum(yh * yh, -1, keepdims=True) + 1e-6)
        if kind == "q":
            yh = yh * (GDN_HEAD_DIM ** -0.5)
        o_ref[:, h * GDN_HEAD_DIM:(h + 1) * GDN_HEAD_DIM] = yh.astype(BF16)


def _gdn_prep(gdn_raw, conv_w, B, S):
    d = D_MODEL
    cols = 2 * GDN_HEAD_DIM
    nblk = d // cols
    raw3 = gdn_raw.reshape(B, S, 3 * d)
    outs = []
    for idx, kind in enumerate(("q", "k", "v")):
        outs.append(pl.pallas_call(
            functools.partial(_gdn_prep_kernel, kind=kind),
            grid=(B, nblk),
            in_specs=[pl.BlockSpec((None, S, cols), lambda b, c, idx=idx: (b, 0, idx * nblk + c)),
                      pl.BlockSpec((CONV_WIDTH, cols), lambda b, c, idx=idx: (0, idx * nblk + c))],
            out_specs=pl.BlockSpec((None, S, cols), lambda b, c: (b, 0, c)),
            out_shape=jax.ShapeDtypeStruct((B, S, d), BF16),
            compiler_params=_params("arbitrary", "arbitrary"),
        )(raw3, conv_w))
    return outs


def _gdn_kernel(q_ref, k_ref, v_ref, sz_ref, col_ref, row_ref, ng_ref, o_ref, state_ref, *, nc):
    @pl.when(pl.program_id(1) == 0)
    def _():
        state_ref[...] = jnp.zeros_like(state_ref)

    C = GDN_CHUNK
    ri = lax.broadcasted_iota(I32, (C, C), 0)
    ci = lax.broadcasted_iota(I32, (C, C), 1)
    incl = ri >= ci
    strict = ri > ci
    eye = (ri == ci).astype(F32)
    ng = ng_ref[...]
    for c in range(nc):
        rows = slice(c * C, (c + 1) * C)
        for h in range(GDN_HEADS):
            lanes = slice(h * GDN_HEAD_DIM, (h + 1) * GDN_HEAD_DIM)
            qh, kh, vh = q_ref[rows, lanes], k_ref[rows, lanes], v_ref[rows, lanes]
            beta_c = col_ref[rows, h:h + 1]
            g_c = col_ref[rows, GDN_HEADS + h:GDN_HEADS + h + 1]
            g_r = row_ref[GDN_HEADS + h:GDN_HEADS + h + 1, rows]
            gap = g_c - g_r
            decay = jnp.where(incl, jnp.exp(jnp.where(incl, gap, 0.0)), 0.0)
            kq = _dot_nt(jnp.concatenate([kh, qh], 0), kh)
            a_mat = jnp.where(strict, kq[:C] * decay, 0.0) * beta_c
            qk = kq[C:] * decay
            p = -a_mat
            t_mat = eye + p
            for _ in range(5):
                pb = p.astype(BF16)
                p = _dot(pb, pb)
                t_mat = t_mat + _dot(t_mat.astype(BF16), p.astype(BF16))
            eg_c = jnp.exp(g_c)
            kf = kh.astype(F32)
            rhs = jnp.concatenate([vh.astype(F32) * beta_c, kf * (beta_c * eg_c)], 1)
            uw = _dot(t_mat.astype(BF16), rhs.astype(BF16))
            u, w = uw[:, :GDN_HEAD_DIM], uw[:, GDN_HEAD_DIM:]
            g_last = g_c[C - 1:C, :]
            q_dec = qh.astype(F32) * eg_c
            k_dec = kf * jnp.exp(g_last - g_c)
            st = state_ref[h]
            sb = st.astype(BF16)
            ws = _dot(jnp.concatenate([w, q_dec], 0).astype(BF16), sb)
            v_new = u - ws[:C]
            vnb = v_new.astype(BF16)
            o = ws[C:] + _dot(qk.astype(BF16), vnb)
            state_ref[h] = st * jnp.exp(g_last) + _dot_tn(k_dec.astype(BF16), vnb)
            o = o * lax.rsqrt(jnp.mean(o * o, -1, keepdims=True) + 1e-6) * ng
            o_ref[rows, lanes] = (o * sz_ref[rows, lanes].astype(F32)).astype(BF16)


def _gated_deltanet(gq, gk, gv, sz, beta_t, g_t, norm_g, B, S):
    d = D_MODEL
    nc = GDN_CHUNKS_PER_STEP
    R = nc * GDN_CHUNK
    row_arr = jnp.concatenate([beta_t, g_t], 0).reshape(2 * GDN_HEADS, B, S).transpose(1, 0, 2)
    col_arr = row_arr.transpose(0, 2, 1)
    blk = pl.BlockSpec((None, R, d), lambda b, i: (b, i, 0))
    out = pl.pallas_call(
        functools.partial(_gdn_kernel, nc=nc),
        grid=(B, S // R),
        in_specs=[blk, blk, blk, blk,
                  pl.BlockSpec((None, R, 2 * GDN_HEADS), lambda b, i: (b, i, 0)),
                  pl.BlockSpec((None, 2 * GDN_HEADS, R), lambda b, i: (b, 0, i)),
                  _const_spec((1, GDN_HEAD_DIM))],
        out_specs=blk,
        out_shape=jax.ShapeDtypeStruct((B, S, d), BF16),
        scratch_shapes=[pltpu.VMEM((GDN_HEADS, GDN_HEAD_DIM, GDN_HEAD_DIM), F32)],
        compiler_params=_params("arbitrary", "arbitrary"),
    )(gq, gk, gv, sz.reshape(B, S, d), col_arr, row_arr, norm_g.reshape(1, GDN_HEAD_DIM))
    return out.reshape(B * S, d)


def _mix_kernel(x_ref, od_ref, og_ref, ga_ref, gb_ref, wd_ref, wg_ref, wm_ref, g_ref, b_ref,
                o_ref, *, alpha):
    yd = _dot(od_ref[...], wd_ref[...])
    yg = _dot(og_ref[...], wg_ref[...])
    m = ga_ref[...].astype(F32) * yd + gb_ref[...].astype(F32) * yg
    r = alpha * x_ref[...] + _dot(m.astype(BF16), wm_ref[...])
    o_ref[...] = _layer_norm(r, g_ref[...], b_ref[...])


def _mix(xf, od, og, sga, sgb, w_diff_o, w_gdn_o, w_mix_o, ln_g, ln_b, alpha):
    T, d = xf.shape
    tm = min(TOKEN_TILE, T)
    tok = _token_spec(tm, d)
    wspec = _const_spec((d, d))
    vec = _const_spec((1, d))
    return pl.pallas_call(
        functools.partial(_mix_kernel, alpha=alpha), grid=(T // tm,),
        in_specs=[tok] * 5 + [wspec] * 3 + [vec] * 2,
        out_specs=tok, out_shape=jax.ShapeDtypeStruct((T, d), F32),
        compiler_params=_params("arbitrary"),
    )(xf, od, og, sga, sgb, w_diff_o.astype(BF16), w_gdn_o.astype(BF16), w_mix_o.astype(BF16),
      ln_g.reshape(1, d), ln_b.reshape(1, d))


def _cross_kernel(x_ref, k_ref, v_ref, wq_ref, wo_ref, g_ref, b_ref, wr_ref, br_ref,
                  o_ref, idx_ref, gate_ref, *, alpha):
    x = x_ref[...]
    q = (_dot(x.astype(BF16), wq_ref[...]) * (MEM_HEAD_DIM ** -0.5)).astype(BF16)
    outs = []
    for h in range(MEM_HEADS):
        lanes = slice(h * MEM_HEAD_DIM, (h + 1) * MEM_HEAD_DIM)
        s = _dot_nt(q[:, lanes], k_ref[:, lanes])
        p = jnp.exp(s - jnp.max(s, -1, keepdims=True))
        p = p / jnp.sum(p, -1, keepdims=True)
        outs.append(_dot(p.astype(BF16), v_ref[:, lanes]))
    o = jnp.concatenate(outs, 1).astype(BF16)
    r = alpha * x + _dot(o, wo_ref[...])
    x2 = _layer_norm(r, g_ref[...], b_ref[...])
    o_ref[...] = x2

    logits = jnp.dot(x2, wr_ref[...], precision=HIGHEST, preferred_element_type=F32) + br_ref[...]
    lane = lax.broadcasted_iota(I32, logits.shape, 1)
    vals, idxs = [], []
    for _ in range(TOP_K):
        mx = jnp.max(logits, -1, keepdims=True)
        ix = jnp.min(jnp.where(logits == mx, lane, N_EXPERTS), -1, keepdims=True)
        vals.append(mx)
        idxs.append(ix)
        logits = jnp.where(lane == ix, -jnp.inf, logits)
    top = jnp.concatenate(vals, 1)
    e = jnp.exp(top - vals[0])
    gate_ref[...] = e / jnp.sum(e, -1, keepdims=True)
    idx_ref[...] = jnp.concatenate(idxs, 1)


def _cross_attention(x1, mem, w_cq, w_ck, w_cv, w_co, ln_g, ln_b, w_router, b_router, B, S, alpha):
    T, d = x1.shape
    M = mem.shape[1]
    w_kv = jnp.concatenate([w_ck, w_cv], 1).astype(BF16)
    tmem = min(TOKEN_TILE, B * M)
    kv = pl.pallas_call(
        _proj_plain_kernel, grid=(B * M // tmem,),
        in_specs=[_token_spec(tmem, d), _const_spec((d, 2 * d))],
        out_specs=_token_spec(tmem, 2 * d),
        out_shape=jax.ShapeDtypeStruct((B * M, 2 * d), BF16),
        compiler_params=_params("arbitrary"),
    )(mem.reshape(B * M, d), w_kv).reshape(B, M, 2 * d)

    tm = min(TOKEN_TILE, S)
    nt = S // tm
    tok = pl.BlockSpec((tm, d), lambda b, i: (b * nt + i, 0))
    small = pl.BlockSpec((tm, TOP_K), lambda b, i: (b * nt + i, 0))
    return pl.pallas_call(
        functools.partial(_cross_kernel, alpha=alpha), grid=(B, nt),
        in_specs=[tok,
                  pl.BlockSpec((None, M, d), lambda b, i: (b, 0, 0)),
                  pl.BlockSpec((None, M, d), lambda b, i: (b, 0, 1)),
                  _const_spec((d, d)), _const_spec((d, d)), _const_spec((1, d)), _const_spec((1, d)),
                  _const_spec((d, N_EXPERTS)), _const_spec((1, N_EXPERTS))],
        out_specs=[tok, small, small],
        out_shape=[jax.ShapeDtypeStruct((T, d), F32), jax.ShapeDtypeStruct((T, TOP_K), I32),
                   jax.ShapeDtypeStruct((T, TOP_K), F32)],
        compiler_params=_params("arbitrary", "arbitrary"),
    )(x1, kv, kv, w_cq.astype(BF16), w_co.astype(BF16), ln_g.reshape(1, d), ln_b.reshape(1, d),
      w_router, b_router.reshape(1, N_EXPERTS))


def _rank_kernel(idx_ref, rank_ref, cnt_ref, run_ref):
    @pl.when(pl.program_id(0) == 0)
    def _():
        run_ref[...] = jnp.zeros_like(run_ref)

    idx = idx_ref[...]
    tm = idx.shape[0]
    lane = lax.broadcasted_iota(I32, (tm, N_EXPERTS), 1)
    hits = [lane == idx[:, k:k + 1] for k in range(TOP_K)]
    onehot = sum(hit.astype(F32) for hit in hits)
    r = lax.broadcasted_iota(I32, (tm, tm), 0)
    c = lax.broadcasted_iota(I32, (tm, tm), 1)
    lower = (r > c).astype(BF16)
    before = _dot(lower, onehot.astype(BF16)) + run_ref[...]
    ranks = [jnp.sum(jnp.where(hit, before, 0.0), -1, keepdims=True) for hit in hits]
    rank_ref[...] = jnp.concatenate(ranks, 1).astype(I32)
    run_ref[...] = run_ref[...] + jnp.sum(onehot, 0, keepdims=True)
    cnt_ref[...] = run_ref[...]


def _dest_kernel(idx_ref, rank_ref, start_ref, dest_ref):
    idx = idx_ref[...]
    lane = lax.broadcasted_iota(I32, (idx.shape[0], N_EXPERTS), 1)
    starts = start_ref[...]
    cols = [jnp.sum(jnp.where(lane == idx[:, k:k + 1], starts, 0), -1, keepdims=True)
            for k in range(TOP_K)]
    dest_ref[...] = rank_ref[...] + jnp.concatenate(cols, 1)


def _row_copy(src_ref, src_row, dst_ref, dst_row, sem):
    return pltpu.make_async_copy(src_ref.at[pl.ds(src_row, 1), :], dst_ref.at[pl.ds(dst_row, 1), :], sem)


def _dispatch_kernel(dest_hbm, x_ref, zeros_hbm, xs_hbm, dest_smem, sem_idx, sem_rows):
    del zeros_hbm
    tm = x_ref.shape[0]
    n = tm * TOP_K
    i = pl.program_id(0)
    cp = pltpu.make_async_copy(dest_hbm.at[pl.ds(pl.multiple_of(i * n, n), n)], dest_smem, sem_idx)
    cp.start()
    cp.wait()

    def issue(j, _):
        _row_copy(x_ref, j // TOP_K, xs_hbm, dest_smem[j], sem_rows).start()
        return 0

    def drain(j, _):
        _row_copy(x_ref, j // TOP_K, xs_hbm, dest_smem[j], sem_rows).wait()
        return 0

    lax.fori_loop(0, n, issue, 0)
    lax.fori_loop(0, n, drain, 0)


def _expert_kernel(te_ref, tv_ref, xs_ref, w1_ref, b1_ref, w2_ref, b2_ref, ys_ref):
    i = pl.program_id(0)

    @pl.when(tv_ref[i] != 0)
    def _():
        ff = w2_ref.shape[0]
        h = _dot(xs_ref[...].astype(BF16), w1_ref[...]) + b1_ref[...]
        h_gate = jnp.minimum(h[:, :ff], SWIGLU_LIMIT)
        h_up = jnp.clip(h[:, ff:], -SWIGLU_LIMIT, SWIGLU_LIMIT)
        act = h_gate * _sigmoid(SWIGLU_ALPHA * h_gate) * (h_up + 1.0)
        ys_ref[...] = _dot(act.astype(BF16), w2_ref[...]) + b2_ref[...]

    @pl.when(tv_ref[i] == 0)
    def _():
        ys_ref[...] = jnp.zeros_like(ys_ref)


def _combine_kernel(dest_hbm, ys_hbm, x_ref, gate_ref, g_ref, b_ref, o_ref,
                    dest_smem, buf, sem_idx, sem_rows, *, alpha):
    tm = x_ref.shape[0]
    n = tm * TOP_K
    i = pl.program_id(0)
    cp = pltpu.make_async_copy(dest_hbm.at[pl.ds(pl.multiple_of(i * n, n), n)], dest_smem, sem_idx)
    cp.start()
    cp.wait()

    def issue(j, _):
        _row_copy(ys_hbm, dest_smem[j], buf.at[j % TOP_K], j // TOP_K, sem_rows).start()
        return 0

    def drain(j, _):
        _row_copy(ys_hbm, dest_smem[j], buf.at[j % TOP_K], j // TOP_K, sem_rows).wait()
        return 0

    lax.fori_loop(0, n, issue, 0)
    lax.fori_loop(0, n, drain, 0)
    gates = gate_ref[...]
    y = sum(gates[:, k:k + 1] * buf[k] for k in range(TOP_K))
    o_ref[...] = _layer_norm(alpha * x_ref[...] + y, g_ref[...], b_ref[...])


def _moe(x2, idx, gates, w1, b1, w2, b2, ln_g, ln_b, alpha):
    T, d = x2.shape
    ff = w2.shape[1]
    tk = T * TOP_K
    bm = EXPERT_TILE
    n_tiles = -(-tk // bm) + N_EXPERTS
    R = n_tiles * bm

    tr = min(TOKEN_TILE, T)
    small = lambda tm: pl.BlockSpec((tm, TOP_K), lambda i: (i, 0))
    rank, counts = pl.pallas_call(
        _rank_kernel, grid=(T // tr,),
        in_specs=[small(tr)], out_specs=[small(tr), _const_spec((1, N_EXPERTS))],
        out_shape=[jax.ShapeDtypeStruct((T, TOP_K), I32), jax.ShapeDtypeStruct((1, N_EXPERTS), F32)],
        scratch_shapes=[pltpu.VMEM((1, N_EXPERTS), F32)],
        compiler_params=_params("arbitrary"),
    )(idx)

    counts = counts.reshape(N_EXPERTS).astype(I32)
    padded = (counts + bm - 1) // bm * bm
    pad_ends = jnp.cumsum(padded)
    pad_starts = pad_ends - padded
    tile_row = jnp.arange(n_tiles, dtype=I32) * bm
    tile_e = jnp.minimum(jnp.searchsorted(pad_ends, tile_row, side="right"), N_EXPERTS - 1).astype(I32)
    tile_valid = (tile_row < pad_ends[-1]).astype(I32)

    dest = pl.pallas_call(
        _dest_kernel, grid=(T // tr,),
        in_specs=[small(tr), small(tr), _const_spec((1, N_EXPERTS))], out_specs=small(tr),
        out_shape=jax.ShapeDtypeStruct((T, TOP_K), I32),
        compiler_params=_params("arbitrary"),
    )(idx, rank, pad_starts.reshape(1, N_EXPERTS)).reshape(tk)

    tm = min(ROUTE_TILE, T)
    n = tm * TOP_K
    any_spec = pl.BlockSpec(memory_space=pl.ANY)
    xs = pl.pallas_call(
        _dispatch_kernel, grid=(T // tm,),
        in_specs=[any_spec, _token_spec(tm, d), any_spec], out_specs=any_spec,
        out_shape=jax.ShapeDtypeStruct((R, d), F32),
        scratch_shapes=[pltpu.SMEM((n,), I32), pltpu.SemaphoreType.DMA(()), pltpu.SemaphoreType.DMA(())],
        input_output_aliases={2: 0},
        compiler_params=_params("arbitrary"),
    )(dest, x2, jnp.zeros((R, d), F32))

    ys = pl.pallas_call(
        _expert_kernel,
        grid_spec=pltpu.PrefetchScalarGridSpec(
            num_scalar_prefetch=2, grid=(n_tiles,),
            in_specs=[pl.BlockSpec((bm, d), lambda i, te, tv: (i, 0)),
                      pl.BlockSpec((None, d, 2 * ff), lambda i, te, tv: (te[i], 0, 0)),
                      pl.BlockSpec((None, 1, 2 * ff), lambda i, te, tv: (te[i], 0, 0)),
                      pl.BlockSpec((None, ff, d), lambda i, te, tv: (te[i], 0, 0)),
                      pl.BlockSpec((None, 1, d), lambda i, te, tv: (te[i], 0, 0))],
            out_specs=pl.BlockSpec((bm, d), lambda i, te, tv: (i, 0))),
        out_shape=jax.ShapeDtypeStruct((R, d), F32),
        compiler_params=_params("arbitrary"),
    )(tile_e, tile_valid, xs, w1.astype(BF16), b1.reshape(N_EXPERTS, 1, 2 * ff),
      w2.astype(BF16), b2.reshape(N_EXPERTS, 1, d))

    return pl.pallas_call(
        functools.partial(_combine_kernel, alpha=alpha), grid=(T // tm,),
        in_specs=[any_spec, any_spec, _token_spec(tm, d), small(tm), _const_spec((1, d)), _const_spec((1, d))],
        out_specs=_token_spec(tm, d),
        out_shape=jax.ShapeDtypeStruct((T, d), F32),
        scratch_shapes=[pltpu.SMEM((n,), I32), pltpu.VMEM((TOP_K, tm, d), F32),
                        pltpu.SemaphoreType.DMA(()), pltpu.SemaphoreType.DMA(())],
        compiler_params=_params("arbitrary"),
    )(dest, ys, x2, gates, ln_g.reshape(1, d), ln_b.reshape(1, d))


def kernel(x, mem, positions, w_in, diff_lambda_q1, diff_lambda_k1, diff_lambda_q2, diff_lambda_k2, diff_subln_g, w_diff_o, gdn_conv_w, gdn_A_log, gdn_dt_bias, gdn_norm_g, w_gdn_o, w_mix_o, ln1_g, ln1_b, w_cq, w_ck, w_cv, w_co, ln2_g, ln2_b, w_router, b_router, w_exp_in, b_exp_in, w_exp_out, b_exp_out, ln3_g, ln3_b):
    B, S, d = x.shape
    depth = w_in.shape[0]
    alpha = (2 * depth) ** 0.25
    rope = _rope_tables(positions)
    xf = x.reshape(B * S, d)
    for l in range(depth):
        lambda_init = 0.8 - 0.6 * math.exp(-0.3 * l)
        dq, dk, dv, gdn_raw, sz, sga, sgb, beta_t, g_t = _input_projections(
            xf, w_in[l], rope, gdn_A_log[l], gdn_dt_bias[l])
        lam_vecs = jnp.stack([diff_lambda_q1[l], diff_lambda_k1[l], diff_lambda_q2[l], diff_lambda_k2[l]])
        od = _diff_attention(dq, dk, dv, lam_vecs, diff_subln_g[l], B, S, lambda_init)
        gq, gk, gv = _gdn_prep(gdn_raw, gdn_conv_w[l], B, S)
        og = _gated_deltanet(gq, gk, gv, sz, beta_t, g_t, gdn_norm_g[l], B, S)
        x1 = _mix(xf, od, og, sga, sgb, w_diff_o[l], w_gdn_o[l], w_mix_o[l], ln1_g[l], ln1_b[l], alpha)
        x2, idx, gates = _cross_attention(x1, mem, w_cq[l], w_ck[l], w_cv[l], w_co[l], ln2_g[l], ln2_b[l],
                                          w_router[l], b_router[l], B, S, alpha)
        xf = _moe(x2, idx, gates, w_exp_in[l], b_exp_in[l], w_exp_out[l], b_exp_out[l],
                  ln3_g[l], ln3_b[l], alpha)
    return xf.reshape(B, S, d)
```

```python
import functools
import math

import jax
import jax.numpy as jnp
from jax import lax
from jax.experimental import pallas as pl
from jax.experimental.pallas import tpu as pltpu

F32 = jnp.float32
BF16 = jnp.bfloat16
I32 = jnp.int32

D_MODEL = 1024
DIFF_HEADS = 8
DIFF_HEAD_DIM = 64
ROPE_THETA = 500000.0
ROT_DIM = DIFF_HEAD_DIM // 4
ROT_HALF = ROT_DIM // 2
GDN_HEADS = 8
GDN_HEAD_DIM = 128
CONV_WIDTH = 4
GDN_CHUNK = 64
MEM_HEADS = 4
MEM_HEAD_DIM = D_MODEL // MEM_HEADS
N_EXPERTS = 32
TOP_K = 4
SWIGLU_LIMIT = 7.0
SWIGLU_ALPHA = 1.702
LANES = 128
VMEM_LIMIT = 48 * 1024 * 1024

TOKEN_TILE = 512
ATTN_Q_TILE = 256
GDN_GROUP = 4
ROUTE_TILE = 256
EXPERT_TILE = 512
ROW_UNROLL = 8
HIGHEST = lax.Precision.HIGHEST


def _params(*sem):
    return pltpu.CompilerParams(dimension_semantics=sem, vmem_limit_bytes=VMEM_LIMIT)


def _dot(a, b):
    return jnp.dot(a, b, preferred_element_type=F32)


def _dot_nt(a, b):
    return lax.dot_general(a, b, (((1,), (1,)), ((), ())), preferred_element_type=F32)


def _dot_tn(a, b):
    return lax.dot_general(a, b, (((0,), (0,)), ((), ())), preferred_element_type=F32)


def _sigmoid(x):
    return 1.0 / (1.0 + jnp.exp(-x))


def _layer_norm(r, g, b):
    mu = jnp.mean(r, -1, keepdims=True)
    d = r - mu
    var = jnp.mean(d * d, -1, keepdims=True)
    return d * lax.rsqrt(var + 1e-5) * g + b


def _rope_kernel(pos_ref, inv_ref, cos_ref, sin_ref):
    ang = pos_ref[...] * inv_ref[...]
    cos_ref[...] = jnp.cos(ang)
    sin_ref[...] = jnp.sin(ang)


def _rope_tables(positions):
    T = positions.size
    inv_freq = jnp.power(ROPE_THETA, -jnp.arange(0, ROT_DIM, 2, dtype=F32) / ROT_DIM)
    per_row = LANES // ROT_HALF
    rows = T // per_row
    pos = jnp.repeat(positions.reshape(T).astype(F32), ROT_HALF).reshape(rows, LANES)
    inv = jnp.tile(inv_freq, per_row).reshape(1, LANES)
    rb = min(rows, 512)
    cos, sin = pl.pallas_call(
        _rope_kernel,
        grid=(rows // rb,),
        in_specs=[pl.BlockSpec((rb, LANES), lambda i: (i, 0)),
                  pl.BlockSpec((1, LANES), lambda i: (0, 0))],
        out_specs=[pl.BlockSpec((rb, LANES), lambda i: (i, 0))] * 2,
        out_shape=[jax.ShapeDtypeStruct((rows, LANES), F32)] * 2,
        compiler_params=_params("arbitrary"),
    )(pos, inv)
    cos = cos.reshape(T, ROT_HALF)
    sin = sin.reshape(T, ROT_HALF)
    rest = DIFF_HEAD_DIM - ROT_DIM
    one = jnp.ones((T, rest), F32)
    zero = jnp.zeros((T, rest), F32)
    z8 = jnp.zeros((T, ROT_HALF), F32)
    cos_t = jnp.concatenate([cos, cos, one] * 2, -1)
    sin_a = jnp.concatenate([-sin, z8, zero] * 2, -1)
    sin_b = jnp.concatenate([z8, sin, zero] * 2, -1)
    return cos_t, sin_a, sin_b


def _proj_diff_kernel(x_ref, w_ref, cos_ref, sa_ref, sb_ref, q_ref, k_ref, v_ref):
    xb = x_ref[...].astype(BF16)
    cos_t, sin_a, sin_b = cos_ref[...], sa_ref[...], sb_ref[...]
    qk_w = DIFF_HEADS * 2 * DIFF_HEAD_DIM
    for part, o_ref in enumerate((q_ref, k_ref)):
        acc = _dot(xb, w_ref[:, part * qk_w:(part + 1) * qk_w])
        for h in range(DIFF_HEADS):
            t = acc[:, h * LANES:(h + 1) * LANES]
            r = (t * cos_t + pltpu.roll(t, LANES - ROT_HALF, 1) * sin_a
                 + pltpu.roll(t, ROT_HALF, 1) * sin_b)
            if part == 0:
                r = r * (DIFF_HEAD_DIM ** -0.5)
            o_ref[:, h * LANES:(h + 1) * LANES] = r.astype(BF16)
    v_ref[...] = _dot(xb, w_ref[:, 2 * qk_w:]).astype(BF16)


def _proj_plain_kernel(x_ref, w_ref, o_ref):
    o_ref[...] = _dot(x_ref[...].astype(BF16), w_ref[...]).astype(BF16)


def _proj_gate_kernel(x_ref, w_ref, sz_ref, ga_ref, gb_ref):
    xb = x_ref[...].astype(BF16)
    z = _dot(xb, w_ref[:, :D_MODEL])
    sz_ref[...] = (z * _sigmoid(z)).astype(BF16)
    ga_ref[...] = _sigmoid(_dot(xb, w_ref[:, D_MODEL:2 * D_MODEL])).astype(BF16)
    gb_ref[...] = _sigmoid(_dot(xb, w_ref[:, 2 * D_MODEL:])).astype(BF16)


def _decay_kernel(x_ref, wt_ref, alog_ref, dtb_ref, beta_ref, g_ref):
    ba = lax.dot_general(wt_ref[...], x_ref[...], (((1,), (1,)), ((), ())),
                         precision=HIGHEST, preferred_element_type=F32)
    b, a = ba[:GDN_HEADS], ba[GDN_HEADS:]
    beta_ref[...] = _sigmoid(b)
    s = a + dtb_ref[...]
    softplus = jnp.maximum(s, 0.0) + jnp.log1p(jnp.exp(-jnp.abs(s)))
    g = -jnp.exp(alog_ref[...]) * softplus
    tm = g.shape[1]
    r = lax.broadcasted_iota(I32, (tm, tm), 0)
    c = lax.broadcasted_iota(I32, (tm, tm), 1)
    upper = ((r <= c) & (r // GDN_CHUNK == c // GDN_CHUNK)).astype(F32)
    g_ref[...] = jnp.dot(g, upper, precision=HIGHEST, preferred_element_type=F32)


def _token_spec(tm, width):
    return pl.BlockSpec((tm, width), lambda i: (i, 0))


def _const_spec(shape):
    return pl.BlockSpec(shape, lambda *_: (0,) * len(shape))


def _input_projections(xf, w_in, rope, a_log, dt_bias):
    T = xf.shape[0]
    tm = min(TOKEN_TILE, T)
    grid = (T // tm,)
    d = D_MODEL
    w_diff = w_in[:, :3 * d].astype(BF16)
    w_gdn = w_in[:, 3 * d:6 * d].astype(BF16)
    w_z = w_in[:, 6 * d:7 * d]
    w_ba_t = w_in[:, 7 * d:7 * d + 2 * GDN_HEADS].T
    w_gate = jnp.concatenate([w_z, w_in[:, 7 * d + 2 * GDN_HEADS:]], 1).astype(BF16)
    x_spec = _token_spec(tm, d)
    act = jax.ShapeDtypeStruct((T, d), BF16)

    dq, dk, dv = pl.pallas_call(
        _proj_diff_kernel, grid=grid,
        in_specs=[x_spec, _const_spec((d, 3 * d))] + [_token_spec(tm, LANES)] * 3,
        out_specs=[_token_spec(tm, d)] * 3, out_shape=[act] * 3,
        compiler_params=_params("arbitrary"),
    )(xf, w_diff, *rope)

    gdn_raw = pl.pallas_call(
        _proj_plain_kernel, grid=grid,
        in_specs=[x_spec, _const_spec((d, 3 * d))],
        out_specs=_token_spec(tm, 3 * d),
        out_shape=jax.ShapeDtypeStruct((T, 3 * d), BF16),
        compiler_params=_params("arbitrary"),
    )(xf, w_gdn)

    sz, sga, sgb = pl.pallas_call(
        _proj_gate_kernel, grid=grid,
        in_specs=[x_spec, _const_spec((d, 3 * d))],
        out_specs=[_token_spec(tm, d)] * 3, out_shape=[act] * 3,
        compiler_params=_params("arbitrary"),
    )(xf, w_gate)

    td = min(256, T)
    lane_spec = pl.BlockSpec((GDN_HEADS, td), lambda i: (0, i))
    beta_t, g_t = pl.pallas_call(
        _decay_kernel, grid=(T // td,),
        in_specs=[_token_spec(td, d), _const_spec((2 * GDN_HEADS, d)),
                  _const_spec((GDN_HEADS, 1)), _const_spec((GDN_HEADS, 1))],
        out_specs=[lane_spec] * 2,
        out_shape=[jax.ShapeDtypeStruct((GDN_HEADS, T), F32)] * 2,
        compiler_params=_params("arbitrary"),
    )(xf, w_ba_t, a_log.reshape(GDN_HEADS, 1), dt_bias.reshape(GDN_HEADS, 1))
    return dq, dk, dv, gdn_raw, sz, sga, sgb, beta_t, g_t


def _diff_attn_kernel(lam_ref, g_ref, q_ref, k_ref, v_ref, o_ref, *, lambda_init):
    tq = q_ref.shape[0]
    qi = pl.program_id(2)
    q = q_ref[...]
    lane = lax.broadcasted_iota(I32, q.shape, 1)
    zero = jnp.zeros_like(q)
    q2 = jnp.concatenate([jnp.where(lane < DIFF_HEAD_DIM, q, zero),
                          jnp.where(lane >= DIFF_HEAD_DIM, q, zero)], 0)

    def block(j, carry, masked):
        m, l, acc = carry
        kb = k_ref[pl.ds(pl.multiple_of(j * tq, tq), tq), :]
        vb = v_ref[pl.ds(pl.multiple_of(j * tq, tq), tq), :]
        s = _dot_nt(q2, kb)
        if masked:
            row = lax.broadcasted_iota(I32, (2 * tq, tq), 0)
            row = jnp.where(row >= tq, row - tq, row)
            col = lax.broadcasted_iota(I32, (2 * tq, tq), 1)
            s = jnp.where(col <= row, s, -jnp.inf)
        m_new = jnp.maximum(m, jnp.max(s, -1, keepdims=True))
        alpha = jnp.exp(m - m_new)
        p = jnp.exp(s - m_new)
        l = alpha * l + jnp.sum(p, -1, keepdims=True)
        acc = alpha * acc + _dot(p.astype(BF16), vb)
        return m_new, l, acc

    init = (jnp.full((2 * tq, 1), -jnp.inf, F32), jnp.zeros((2 * tq, 1), F32),
            jnp.zeros((2 * tq, LANES), F32))
    carry = lax.fori_loop(0, qi, lambda j, c: block(j, c, False), init)
    m, l, acc = block(qi, carry, True)
    o = acc / l
    lv = lam_ref[...]
    lam = (jnp.exp(jnp.sum(lv[0:1] * lv[1:2], -1, keepdims=True))
           - jnp.exp(jnp.sum(lv[2:3] * lv[3:4], -1, keepdims=True)) + lambda_init)
    o = o[:tq] - lam * o[tq:]
    o = o * lax.rsqrt(jnp.mean(o * o, -1, keepdims=True) + 1e-6) * g_ref[...]
    o_ref[...] = (o * (1.0 - lambda_init)).astype(BF16)


def _diff_attention(dq, dk, dv, lam_vecs, subln_g, B, S, lambda_init):
    tq = min(ATTN_Q_TILE, S)
    d = D_MODEL
    q3, k3, v3 = (t.reshape(B, S, d) for t in (dq, dk, dv))
    out = pl.pallas_call(
        functools.partial(_diff_attn_kernel, lambda_init=lambda_init),
        grid=(B, DIFF_HEADS, S // tq),
        in_specs=[_const_spec((4, DIFF_HEAD_DIM)), _const_spec((1, LANES)),
                  pl.BlockSpec((None, tq, LANES), lambda b, h, i: (b, i, h)),
                  pl.BlockSpec((None, S, LANES), lambda b, h, i: (b, 0, h)),
                  pl.BlockSpec((None, S, LANES), lambda b, h, i: (b, 0, h))],
        out_specs=pl.BlockSpec((None, tq, LANES), lambda b, h, i: (b, i, h)),
        out_shape=jax.ShapeDtypeStruct((B, S, d), BF16),
        compiler_params=_params("arbitrary", "arbitrary", "arbitrary"),
    )(lam_vecs, subln_g.reshape(1, LANES), q3, k3, v3)
    return out.reshape(B * S, d)


def _gdn_prep_kernel(x_ref, cw_ref, o_ref, *, kind):
    x = x_ref[...].astype(F32)
    row = lax.broadcasted_iota(I32, x.shape, 0)
    y = x * cw_ref[CONV_WIDTH - 1:CONV_WIDTH, :]
    for j in range(1, CONV_WIDTH):
        shifted = jnp.where(row >= j, pltpu.roll(x, j, 0), 0.0)
        y = y + shifted * cw_ref[CONV_WIDTH - 1 - j:CONV_WIDTH - j, :]
    y = y * _sigmoid(y)
    if kind == "v":
        o_ref[...] = y.astype(BF16)
        return
    for h in range(x.shape[1] // GDN_HEAD_DIM):
        yh = y[:, h * GDN_HEAD_DIM:(h + 1) * GDN_HEAD_DIM]
        yh = yh * lax.rsqrt(jnp.sum(yh * yh, -1, keepdims=True) + 1e-6)
        if kind == "q":
            yh = yh * (GDN_HEAD_DIM ** -0.5)
        o_ref[:, h * GDN_HEAD_DIM:(h + 1) * GDN_HEAD_DIM] = yh.astype(BF16)


def _gdn_prep(gdn_raw, conv_w, B, S):
    d = D_MODEL
    cols = 2 * GDN_HEAD_DIM
    nblk = d // cols
    raw3 = gdn_raw.reshape(B, S, 3 * d)
    outs = []
    for idx, kind in enumerate(("q", "k", "v")):
        outs.append(pl.pallas_call(
            functools.partial(_gdn_prep_kernel, kind=kind),
            grid=(B, nblk),
            in_specs=[pl.BlockSpec((None, S, cols), lambda b, c, idx=idx: (b, 0, idx * nblk + c)),
                      pl.BlockSpec((CONV_WIDTH, cols), lambda b, c, idx=idx: (0, idx * nblk + c))],
            out_specs=pl.BlockSpec((None, S, cols), lambda b, c: (b, 0, c)),
            out_shape=jax.ShapeDtypeStruct((B, S, d), BF16),
            compiler_params=_params("arbitrary", "arbitrary"),
        )(raw3, conv_w))
    return outs


def _gdn_local_kernel(q_ref, k_ref, v_ref, col_ref, row_ref,
                      u_ref, w_ref, qd_ref, kd_ref, qk_ref, gl_ref):
    C = GDN_CHUNK
    R = q_ref.shape[0]
    ri = lax.broadcasted_iota(I32, (R, R), 0)
    ci = lax.broadcasted_iota(I32, (R, R), 1)
    same = (ri // C) == (ci // C)
    incl = same & (ri >= ci)
    strict = same & (ri > ci)
    eye = (ri == ci).astype(F32)
    for h in range(GDN_HEADS):
        lanes = slice(h * GDN_HEAD_DIM, (h + 1) * GDN_HEAD_DIM)
        qh, kh, vh = q_ref[:, lanes], k_ref[:, lanes], v_ref[:, lanes]
        beta_c = col_ref[:, h:h + 1]
        g_c = col_ref[:, GDN_HEADS + h:GDN_HEADS + h + 1]
        g_r = row_ref[GDN_HEADS + h:GDN_HEADS + h + 1, :]
        decay = jnp.where(incl, jnp.exp(jnp.where(incl, g_c - g_r, 0.0)), 0.0)
        p = -(jnp.where(strict, _dot_nt(kh, kh) * decay, 0.0) * beta_c)
        t_mat = eye + p
        for _ in range(5):
            pb = p.astype(BF16)
            p = _dot(pb, pb)
            t_mat = t_mat + _dot(t_mat.astype(BF16), p.astype(BF16))
        eg_c = jnp.exp(g_c)
        kf = kh.astype(F32)
        rhs = jnp.concatenate([vh.astype(F32) * beta_c, kf * (beta_c * eg_c)], 1)
        uw = _dot(t_mat.astype(BF16), rhs.astype(BF16))
        u_ref[:, lanes] = uw[:, :GDN_HEAD_DIM].astype(BF16)
        w_ref[:, lanes] = uw[:, GDN_HEAD_DIM:].astype(BF16)
        qd_ref[:, lanes] = (qh.astype(F32) * eg_c).astype(BF16)
        for c in range(R // C):
            rows = slice(c * C, (c + 1) * C)
            g_last = g_c[(c + 1) * C - 1:(c + 1) * C, :]
            kd_ref[rows, lanes] = (kf[rows] * jnp.exp(g_last - g_c[rows])).astype(BF16)
            gl_ref[c, h:h + 1, :] = jnp.broadcast_to(jnp.exp(g_last), (1, GDN_HEAD_DIM))
            qk_ref[c, h] = (_dot_nt(qh[rows], kh[rows]) * decay[rows, rows]).astype(BF16)


def _gdn_scan_kernel(u_ref, w_ref, qd_ref, kd_ref, qk_ref, gl_ref, sz_ref, ng_ref, o_ref, state_ref):
    @pl.when(pl.program_id(1) == 0)
    def _():
        state_ref[...] = jnp.zeros_like(state_ref)

    C = GDN_CHUNK
    ng = ng_ref[...]
    heads = range(GDN_HEADS)
    lanes = [slice(h * GDN_HEAD_DIM, (h + 1) * GDN_HEAD_DIM) for h in heads]
    for c in range(u_ref.shape[0] // C):
        rows = slice(c * C, (c + 1) * C)
        st = [state_ref[h] for h in heads]
        ws = [_dot(jnp.concatenate([w_ref[rows, lanes[h]], qd_ref[rows, lanes[h]]], 0),
                   st[h].astype(BF16)) for h in heads]
        vnb = [(u_ref[rows, lanes[h]].astype(F32) - ws[h][:C]).astype(BF16) for h in heads]
        for h in heads:
            state_ref[h] = st[h] * gl_ref[c, h:h + 1, :] + _dot_tn(kd_ref[rows, lanes[h]], vnb[h])
        for h in heads:
            o = ws[h][C:] + _dot(qk_ref[c, h], vnb[h])
            o = o * lax.rsqrt(jnp.mean(o * o, -1, keepdims=True) + 1e-6) * ng
            o_ref[rows, lanes[h]] = (o * sz_ref[rows, lanes[h]].astype(F32)).astype(BF16)


def _gated_deltanet(gq, gk, gv, sz, beta_t, g_t, norm_g, B, S):
    d = D_MODEL
    C = GDN_CHUNK
    R = min(GDN_GROUP * C, S)
    nc = R // C
    row_arr = jnp.concatenate([beta_t, g_t], 0).reshape(2 * GDN_HEADS, B, S).transpose(1, 0, 2)
    col_arr = row_arr.transpose(0, 2, 1)
    blk = pl.BlockSpec((None, R, d), lambda b, i: (b, i, 0))
    qk_blk = pl.BlockSpec((None, nc, GDN_HEADS, C, C), lambda b, i: (b, i, 0, 0, 0))
    gl_blk = pl.BlockSpec((None, nc, GDN_HEADS, GDN_HEAD_DIM), lambda b, i: (b, i, 0, 0))
    act = jax.ShapeDtypeStruct((B, S, d), BF16)
    u, w, qd, kd, qk, gl = pl.pallas_call(
        _gdn_local_kernel,
        grid=(B, S // R),
        in_specs=[blk, blk, blk,
                  pl.BlockSpec((None, R, 2 * GDN_HEADS), lambda b, i: (b, i, 0)),
                  pl.BlockSpec((None, 2 * GDN_HEADS, R), lambda b, i: (b, 0, i))],
        out_specs=[blk, blk, blk, blk, qk_blk, gl_blk],
        out_shape=[act, act, act, act,
                   jax.ShapeDtypeStruct((B, S // C, GDN_HEADS, C, C), BF16),
                   jax.ShapeDtypeStruct((B, S // C, GDN_HEADS, GDN_HEAD_DIM), F32)],
        compiler_params=_params("arbitrary", "arbitrary"),
    )(gq, gk, gv, col_arr, row_arr)
    out = pl.pallas_call(
        _gdn_scan_kernel,
        grid=(B, S // R),
        in_specs=[blk, blk, blk, blk, qk_blk, gl_blk, blk, _const_spec((1, GDN_HEAD_DIM))],
        out_specs=blk,
        out_shape=act,
        scratch_shapes=[pltpu.VMEM((GDN_HEADS, GDN_HEAD_DIM, GDN_HEAD_DIM), F32)],
        compiler_params=_params("arbitrary", "arbitrary"),
    )(u, w, qd, kd, qk, gl, sz.reshape(B, S, d), norm_g.reshape(1, GDN_HEAD_DIM))
    return out.reshape(B * S, d)


def _mix_kernel(x_ref, od_ref, og_ref, ga_ref, gb_ref, wd_ref, wg_ref, wm_ref, g_ref, b_ref,
                o_ref, *, alpha):
    yd = _dot(od_ref[...], wd_ref[...])
    yg = _dot(og_ref[...], wg_ref[...])
    m = ga_ref[...].astype(F32) * yd + gb_ref[...].astype(F32) * yg
    r = alpha * x_ref[...] + _dot(m.astype(BF16), wm_ref[...])
    o_ref[...] = _layer_norm(r, g_ref[...], b_ref[...])


def _mix(xf, od, og, sga, sgb, w_diff_o, w_gdn_o, w_mix_o, ln_g, ln_b, alpha):
    T, d = xf.shape
    tm = min(TOKEN_TILE, T)
    tok = _token_spec(tm, d)
    wspec = _const_spec((d, d))
    vec = _const_spec((1, d))
    return pl.pallas_call(
        functools.partial(_mix_kernel, alpha=alpha), grid=(T // tm,),
        in_specs=[tok] * 5 + [wspec] * 3 + [vec] * 2,
        out_specs=tok, out_shape=jax.ShapeDtypeStruct((T, d), F32),
        compiler_params=_params("arbitrary"),
    )(xf, od, og, sga, sgb, w_diff_o.astype(BF16), w_gdn_o.astype(BF16), w_mix_o.astype(BF16),
      ln_g.reshape(1, d), ln_b.reshape(1, d))


def _cross_kernel(x_ref, k_ref, v_ref, wq_ref, wo_ref, g_ref, b_ref, wr_ref, br_ref,
                  o_ref, idx_ref, gate_ref, *, alpha):
    x = x_ref[...]
    q = (_dot(x.astype(BF16), wq_ref[...]) * (MEM_HEAD_DIM ** -0.5)).astype(BF16)
    outs = []
    for h in range(MEM_HEADS):
        lanes = slice(h * MEM_HEAD_DIM, (h + 1) * MEM_HEAD_DIM)
        s = _dot_nt(q[:, lanes], k_ref[:, lanes])
        p = jnp.exp(s - jnp.max(s, -1, keepdims=True))
        p = p / jnp.sum(p, -1, keepdims=True)
        outs.append(_dot(p.astype(BF16), v_ref[:, lanes]))
    o = jnp.concatenate(outs, 1).astype(BF16)
    r = alpha * x + _dot(o, wo_ref[...])
    x2 = _layer_norm(r, g_ref[...], b_ref[...])
    o_ref[...] = x2

    logits = jnp.dot(x2, wr_ref[...], precision=HIGHEST, preferred_element_type=F32) + br_ref[...]
    lane = lax.broadcasted_iota(I32, logits.shape, 1)
    vals, idxs = [], []
    for _ in range(TOP_K):
        mx = jnp.max(logits, -1, keepdims=True)
        ix = jnp.min(jnp.where(logits == mx, lane, N_EXPERTS), -1, keepdims=True)
        vals.append(mx)
        idxs.append(ix)
        logits = jnp.where(lane == ix, -jnp.inf, logits)
    top = jnp.concatenate(vals, 1)
    e = jnp.exp(top - vals[0])
    gate_ref[...] = e / jnp.sum(e, -1, keepdims=True)
    idx_ref[...] = jnp.concatenate(idxs, 1)


def _cross_attention(x1, mem, w_cq, w_ck, w_cv, w_co, ln_g, ln_b, w_router, b_router, B, S, alpha):
    T, d = x1.shape
    M = mem.shape[1]
    w_kv = jnp.concatenate([w_ck, w_cv], 1).astype(BF16)
    tmem = min(TOKEN_TILE, B * M)
    kv = pl.pallas_call(
        _proj_plain_kernel, grid=(B * M // tmem,),
        in_specs=[_token_spec(tmem, d), _const_spec((d, 2 * d))],
        out_specs=_token_spec(tmem, 2 * d),
        out_shape=jax.ShapeDtypeStruct((B * M, 2 * d), BF16),
        compiler_params=_params("arbitrary"),
    )(mem.reshape(B * M, d), w_kv).reshape(B, M, 2 * d)

    tm = min(TOKEN_TILE, S)
    nt = S // tm
    tok = pl.BlockSpec((tm, d), lambda b, i: (b * nt + i, 0))
    small = pl.BlockSpec((tm, TOP_K), lambda b, i: (b * nt + i, 0))
    return pl.pallas_call(
        functools.partial(_cross_kernel, alpha=alpha), grid=(B, nt),
        in_specs=[tok,
                  pl.BlockSpec((None, M, d), lambda b, i: (b, 0, 0)),
                  pl.BlockSpec((None, M, d), lambda b, i: (b, 0, 1)),
                  _const_spec((d, d)), _const_spec((d, d)), _const_spec((1, d)), _const_spec((1, d)),
                  _const_spec((d, N_EXPERTS)), _const_spec((1, N_EXPERTS))],
        out_specs=[tok, small, small],
        out_shape=[jax.ShapeDtypeStruct((T, d), F32), jax.ShapeDtypeStruct((T, TOP_K), I32),
                   jax.ShapeDtypeStruct((T, TOP_K), F32)],
        compiler_params=_params("arbitrary", "arbitrary"),
    )(x1, kv, kv, w_cq.astype(BF16), w_co.astype(BF16), ln_g.reshape(1, d), ln_b.reshape(1, d),
      w_router, b_router.reshape(1, N_EXPERTS))


def _rank_kernel(idx_ref, rank_ref, cnt_ref, run_ref):
    @pl.when(pl.program_id(0) == 0)
    def _():
        run_ref[...] = jnp.zeros_like(run_ref)

    idx = idx_ref[...]
    tm = idx.shape[0]
    lane = lax.broadcasted_iota(I32, (tm, N_EXPERTS), 1)
    hits = [lane == idx[:, k:k + 1] for k in range(TOP_K)]
    onehot = sum(hit.astype(F32) for hit in hits)
    r = lax.broadcasted_iota(I32, (tm, tm), 0)
    c = lax.broadcasted_iota(I32, (tm, tm), 1)
    lower = (r > c).astype(BF16)
    before = _dot(lower, onehot.astype(BF16)) + run_ref[...]
    ranks = [jnp.sum(jnp.where(hit, before, 0.0), -1, keepdims=True) for hit in hits]
    rank_ref[...] = jnp.concatenate(ranks, 1).astype(I32)
    run_ref[...] = run_ref[...] + jnp.sum(onehot, 0, keepdims=True)
    cnt_ref[...] = run_ref[...]


def _dest_kernel(idx_ref, rank_ref, start_ref, dest_ref):
    idx = idx_ref[...]
    lane = lax.broadcasted_iota(I32, (idx.shape[0], N_EXPERTS), 1)
    starts = start_ref[...]
    cols = [jnp.sum(jnp.where(lane == idx[:, k:k + 1], starts, 0), -1, keepdims=True)
            for k in range(TOP_K)]
    dest_ref[...] = rank_ref[...] + jnp.concatenate(cols, 1)


def _row_copy(src_ref, src_row, dst_ref, dst_row, sem):
    return pltpu.make_async_copy(src_ref.at[pl.ds(src_row, 1), :], dst_ref.at[pl.ds(dst_row, 1), :], sem)


def _index_copy(dest_hbm, dest_smem, sem_idx, step, slot, n):
    return pltpu.make_async_copy(dest_hbm.at[pl.ds(pl.multiple_of(step * n, n), n)],
                                 dest_smem.at[pl.ds(pl.multiple_of(slot * n, n), n)], sem_idx.at[slot])


def _dispatch_kernel(dest_hbm, x_hbm, zeros_hbm, xs_hbm, dest_smem, sem_idx, sem_rows, *, tm):
    del zeros_hbm
    n = tm * TOP_K
    i = pl.program_id(0)
    last = pl.num_programs(0) - 1
    slot = i % 2

    @pl.when(i == 0)
    def _():
        _index_copy(dest_hbm, dest_smem, sem_idx, 0, 0, n).start()

    _index_copy(dest_hbm, dest_smem, sem_idx, i, slot, n).wait()

    @pl.when(i < last)
    def _():
        _index_copy(dest_hbm, dest_smem, sem_idx, i + 1, 1 - slot, n).start()

    base = slot * n
    tok0 = i * tm

    def issue(jj, _):
        for u in range(ROW_UNROLL):
            j = jj * ROW_UNROLL + u
            tok = tok0 + jj * (ROW_UNROLL // TOP_K) + u // TOP_K
            _row_copy(x_hbm, tok, xs_hbm, dest_smem[base + j], sem_rows).start(priority=u % 2)
        return 0

    lax.fori_loop(0, n // ROW_UNROLL, issue, 0)

    def wait_tile():
        for _ in range(TOP_K):
            pltpu.make_async_copy(x_hbm.at[pl.ds(0, tm), :], xs_hbm.at[pl.ds(0, tm), :], sem_rows).wait()

    @pl.when(i > 0)
    def _():
        wait_tile()

    @pl.when(i == last)
    def _():
        wait_tile()


def _expert_kernel(te_ref, tv_ref, xs_ref, w1_ref, b1_ref, w2_ref, b2_ref, ys_ref):
    i = pl.program_id(0)

    @pl.when(tv_ref[i] != 0)
    def _():
        ff = w2_ref.shape[0]
        h = _dot(xs_ref[...].astype(BF16), w1_ref[...]) + b1_ref[...]
        h_gate = jnp.minimum(h[:, :ff], SWIGLU_LIMIT)
        h_up = jnp.clip(h[:, ff:], -SWIGLU_LIMIT, SWIGLU_LIMIT)
        act = h_gate * _sigmoid(SWIGLU_ALPHA * h_gate) * (h_up + 1.0)
        ys_ref[...] = _dot(act.astype(BF16), w2_ref[...]) + b2_ref[...]

    @pl.when(tv_ref[i] == 0)
    def _():
        ys_ref[...] = jnp.zeros_like(ys_ref)


def _combine_kernel(dest_hbm, ys_hbm, x_ref, gate_ref, g_ref, b_ref, o_ref,
                    dest_smem, buf, sem_idx, sem_rows, *, alpha):
    tm = x_ref.shape[0]
    n = tm * TOP_K
    i = pl.program_id(0)
    last = pl.num_programs(0) - 1
    slot = i % 2

    def gather_tile(step, s):
        cp = _index_copy(dest_hbm, dest_smem, sem_idx, step, s, n)
        cp.start()
        cp.wait()
        base = s * n

        def issue(jj, _):
            for u in range(ROW_UNROLL):
                j = jj * ROW_UNROLL + u
                tok = jj * (ROW_UNROLL // TOP_K) + u // TOP_K
                _row_copy(ys_hbm, dest_smem[base + j], buf.at[s, u % TOP_K], tok,
                          sem_rows.at[s]).start(priority=u % 2)
            return 0

        lax.fori_loop(0, n // ROW_UNROLL, issue, 0)

    @pl.when(i == 0)
    def _():
        gather_tile(0, 0)

    @pl.when(i < last)
    def _():
        gather_tile(i + 1, 1 - slot)

    for k in range(TOP_K):
        pltpu.make_async_copy(ys_hbm.at[pl.ds(0, tm), :], buf.at[slot, k], sem_rows.at[slot]).wait()
    gates = gate_ref[...]
    y = sum(gates[:, k:k + 1] * buf[slot, k] for k in range(TOP_K))
    o_ref[...] = _layer_norm(alpha * x_ref[...] + y, g_ref[...], b_ref[...])


def _moe(x2, idx, gates, w1, b1, w2, b2, ln_g, ln_b, alpha):
    T, d = x2.shape
    ff = w2.shape[1]
    tk = T * TOP_K
    bm = EXPERT_TILE
    n_tiles = -(-tk // bm) + N_EXPERTS
    R = n_tiles * bm

    tr = min(TOKEN_TILE, T)
    small = lambda tm: pl.BlockSpec((tm, TOP_K), lambda i: (i, 0))
    rank, counts = pl.pallas_call(
        _rank_kernel, grid=(T // tr,),
        in_specs=[small(tr)], out_specs=[small(tr), _const_spec((1, N_EXPERTS))],
        out_shape=[jax.ShapeDtypeStruct((T, TOP_K), I32), jax.ShapeDtypeStruct((1, N_EXPERTS), F32)],
        scratch_shapes=[pltpu.VMEM((1, N_EXPERTS), F32)],
        compiler_params=_params("arbitrary"),
    )(idx)

    counts = counts.reshape(N_EXPERTS).astype(I32)
    padded = (counts + bm - 1) // bm * bm
    pad_ends = jnp.cumsum(padded)
    pad_starts = pad_ends - padded
    tile_row = jnp.arange(n_tiles, dtype=I32) * bm
    tile_e = jnp.minimum(jnp.searchsorted(pad_ends, tile_row, side="right"), N_EXPERTS - 1).astype(I32)
    tile_valid = (tile_row < pad_ends[-1]).astype(I32)

    dest = pl.pallas_call(
        _dest_kernel, grid=(T // tr,),
        in_specs=[small(tr), small(tr), _const_spec((1, N_EXPERTS))], out_specs=small(tr),
        out_shape=jax.ShapeDtypeStruct((T, TOP_K), I32),
        compiler_params=_params("arbitrary"),
    )(idx, rank, pad_starts.reshape(1, N_EXPERTS)).reshape(tk)

    tm = min(ROUTE_TILE, T)
    n = tm * TOP_K
    any_spec = pl.BlockSpec(memory_space=pl.ANY)
    xs = pl.pallas_call(
        functools.partial(_dispatch_kernel, tm=tm), grid=(T // tm,),
        in_specs=[any_spec, any_spec, any_spec], out_specs=any_spec,
        out_shape=jax.ShapeDtypeStruct((R, d), F32),
        scratch_shapes=[pltpu.SMEM((2 * n,), I32), pltpu.SemaphoreType.DMA((2,)), pltpu.SemaphoreType.DMA(())],
        input_output_aliases={2: 0},
        compiler_params=_params("arbitrary"),
    )(dest, x2, jnp.zeros((R, d), F32))

    ys = pl.pallas_call(
        _expert_kernel,
        grid_spec=pltpu.PrefetchScalarGridSpec(
            num_scalar_prefetch=2, grid=(n_tiles,),
            in_specs=[pl.BlockSpec((bm, d), lambda i, te, tv: (i, 0)),
                      pl.BlockSpec((None, d, 2 * ff), lambda i, te, tv: (te[i], 0, 0)),
                      pl.BlockSpec((None, 1, 2 * ff), lambda i, te, tv: (te[i], 0, 0)),
                      pl.BlockSpec((None, ff, d), lambda i, te, tv: (te[i], 0, 0)),
                      pl.BlockSpec((None, 1, d), lambda i, te, tv: (te[i], 0, 0))],
            out_specs=pl.BlockSpec((bm, d), lambda i, te, tv: (i, 0))),
        out_shape=jax.ShapeDtypeStruct((R, d), F32),
        compiler_params=_params("arbitrary"),
    )(tile_e, tile_valid, xs, w1.astype(BF16), b1.reshape(N_EXPERTS, 1, 2 * ff),
      w2.astype(BF16), b2.reshape(N_EXPERTS, 1, d))

    return pl.pallas_call(
        functools.partial(_combine_kernel, alpha=alpha), grid=(T // tm,),
        in_specs=[any_spec, any_spec, _token_spec(tm, d), small(tm), _const_spec((1, d)), _const_spec((1, d))],
        out_specs=_token_spec(tm, d),
        out_shape=jax.ShapeDtypeStruct((T, d), F32),
        scratch_shapes=[pltpu.SMEM((2 * n,), I32), pltpu.VMEM((2, TOP_K, tm, d), F32),
                        pltpu.SemaphoreType.DMA((2,)), pltpu.SemaphoreType.DMA((2,))],
        compiler_params=_params("arbitrary"),
    )(dest, ys, x2, gates, ln_g.reshape(1, d), ln_b.reshape(1, d))


def kernel(x, mem, positions, w_in, diff_lambda_q1, diff_lambda_k1, diff_lambda_q2, diff_lambda_k2, diff_subln_g, w_diff_o, gdn_conv_w, gdn_A_log, gdn_dt_bias, gdn_norm_g, w_gdn_o, w_mix_o, ln1_g, ln1_b, w_cq, w_ck, w_cv, w_co, ln2_g, ln2_b, w_router, b_router, w_exp_in, b_exp_in, w_exp_out, b_exp_out, ln3_g, ln3_b):
    B, S, d = x.shape
    depth = w_in.shape[0]
    alpha = (2 * depth) ** 0.25
    rope = _rope_tables(positions)
    xf = x.reshape(B * S, d)
    for l in range(depth):
        lambda_init = 0.8 - 0.6 * math.exp(-0.3 * l)
        dq, dk, dv, gdn_raw, sz, sga, sgb, beta_t, g_t = _input_projections(
            xf, w_in[l], rope, gdn_A_log[l], gdn_dt_bias[l])
        lam_vecs = jnp.stack([diff_lambda_q1[l], diff_lambda_k1[l], diff_lambda_q2[l], diff_lambda_k2[l]])
        od = _diff_attention(dq, dk, dv, lam_vecs, diff_subln_g[l], B, S, lambda_init)
        gq, gk, gv = _gdn_prep(gdn_raw, gdn_conv_w[l], B, S)
        og = _gated_deltanet(gq, gk, gv, sz, beta_t, g_t, gdn_norm_g[l], B, S)
        x1 = _mix(xf, od, og, sga, sgb, w_diff_o[l], w_gdn_o[l], w_mix_o[l], ln1_g[l], ln1_b[l], alpha)
        x2, idx, gates = _cross_attention(x1, mem, w_cq[l], w_ck[l], w_cv[l], w_co[l], ln2_g[l], ln2_b[l],
                                          w_router[l], b_router[l], B, S, alpha)
        xf = _moe(x2, idx, gates, w_exp_in[l], b_exp_in[l], w_exp_out[l], b_exp_out[l],
                  ln3_g[l], ln3_b[l], alpha)
    return xf.reshape(B, S, d)
```

```python
import functools
import math

import jax
import jax.numpy as jnp
from jax import lax
from jax.experimental import pallas as pl
from jax.experimental.pallas import tpu as pltpu

F32 = jnp.float32
BF16 = jnp.bfloat16
I32 = jnp.int32

D_MODEL = 1024
DIFF_HEADS = 8
DIFF_HEAD_DIM = 64
ROPE_THETA = 500000.0
ROT_DIM = DIFF_HEAD_DIM // 4
ROT_HALF = ROT_DIM // 2
GDN_HEADS = 8
GDN_HEAD_DIM = 128
CONV_WIDTH = 4
GDN_CHUNK = 64
MEM_HEADS = 4
MEM_HEAD_DIM = D_MODEL // MEM_HEADS
N_EXPERTS = 32
TOP_K = 4
SWIGLU_LIMIT = 7.0
SWIGLU_ALPHA = 1.702
LANES = 128
VMEM_LIMIT = 48 * 1024 * 1024

TOKEN_TILE = 512
ATTN_Q_TILE = 256
GDN_GROUP = 4
ROUTE_TILE = 256
EXPERT_TILE = 512
ROW_UNROLL = 8
HIGHEST = lax.Precision.HIGHEST


def _params(*sem):
    return pltpu.CompilerParams(dimension_semantics=sem, vmem_limit_bytes=VMEM_LIMIT)


def _dot(a, b):
    return jnp.dot(a, b, preferred_element_type=F32)


def _dot_nt(a, b):
    return lax.dot_general(a, b, (((1,), (1,)), ((), ())), preferred_element_type=F32)


def _dot_tn(a, b):
    return lax.dot_general(a, b, (((0,), (0,)), ((), ())), preferred_element_type=F32)


def _sigmoid(x):
    return 1.0 / (1.0 + jnp.exp(-x))


def _layer_norm(r, g, b):
    mu = jnp.mean(r, -1, keepdims=True)
    d = r - mu
    var = jnp.mean(d * d, -1, keepdims=True)
    return d * lax.rsqrt(var + 1e-5) * g + b


def _rope_kernel(pos_ref, inv_ref, cos_ref, sin_ref):
    ang = pos_ref[...] * inv_ref[...]
    cos_ref[...] = jnp.cos(ang)
    sin_ref[...] = jnp.sin(ang)


def _rope_tables(positions):
    T = positions.size
    inv_freq = jnp.power(ROPE_THETA, -jnp.arange(0, ROT_DIM, 2, dtype=F32) / ROT_DIM)
    per_row = LANES // ROT_HALF
    rows = T // per_row
    pos = jnp.repeat(positions.reshape(T).astype(F32), ROT_HALF).reshape(rows, LANES)
    inv = jnp.tile(inv_freq, per_row).reshape(1, LANES)
    rb = min(rows, 512)
    cos, sin = pl.pallas_call(
        _rope_kernel,
        grid=(rows // rb,),
        in_specs=[pl.BlockSpec((rb, LANES), lambda i: (i, 0)),
                  pl.BlockSpec((1, LANES), lambda i: (0, 0))],
        out_specs=[pl.BlockSpec((rb, LANES), lambda i: (i, 0))] * 2,
        out_shape=[jax.ShapeDtypeStruct((rows, LANES), F32)] * 2,
        compiler_params=_params("arbitrary"),
    )(pos, inv)
    cos = cos.reshape(T, ROT_HALF)
    sin = sin.reshape(T, ROT_HALF)
    rest = DIFF_HEAD_DIM - ROT_DIM
    one = jnp.ones((T, rest), F32)
    zero = jnp.zeros((T, rest), F32)
    z8 = jnp.zeros((T, ROT_HALF), F32)
    cos_t = jnp.concatenate([cos, cos, one] * 2, -1)
    sin_a = jnp.concatenate([-sin, z8, zero] * 2, -1)
    sin_b = jnp.concatenate([z8, sin, zero] * 2, -1)
    return cos_t, sin_a, sin_b


def _proj_diff_kernel(x_ref, w_ref, wvt_ref, cos_ref, sa_ref, sb_ref, q_ref, k_ref, vt_ref):
    xb = x_ref[...].astype(BF16)
    cos_t, sin_a, sin_b = cos_ref[...], sa_ref[...], sb_ref[...]
    qk_w = DIFF_HEADS * 2 * DIFF_HEAD_DIM
    for part, o_ref in enumerate((q_ref, k_ref)):
        acc = _dot(xb, w_ref[:, part * qk_w:(part + 1) * qk_w])
        for h in range(DIFF_HEADS):
            t = acc[:, h * LANES:(h + 1) * LANES]
            r = (t * cos_t + pltpu.roll(t, LANES - ROT_HALF, 1) * sin_a
                 + pltpu.roll(t, ROT_HALF, 1) * sin_b)
            if part == 0:
                r = r * (DIFF_HEAD_DIM ** -0.5 * math.log2(math.e))
            o_ref[:, h * LANES:(h + 1) * LANES] = r.astype(BF16)
    vt = _dot_nt(wvt_ref[...], xb)
    tk = vt_ref.shape[-1]
    for h in range(DIFF_HEADS):
        for c in range(vt_ref.shape[1]):
            vt_ref[h, c] = vt[h * LANES:(h + 1) * LANES, c * tk:(c + 1) * tk].astype(BF16)


def _proj_plain_kernel(x_ref, w_ref, o_ref):
    o_ref[...] = _dot(x_ref[...].astype(BF16), w_ref[...]).astype(BF16)


def _proj_gate_kernel(x_ref, w_ref, sz_ref, ga_ref, gb_ref):
    xb = x_ref[...].astype(BF16)
    z = _dot(xb, w_ref[:, :D_MODEL])
    sz_ref[...] = (z * _sigmoid(z)).astype(BF16)
    ga_ref[...] = _sigmoid(_dot(xb, w_ref[:, D_MODEL:2 * D_MODEL])).astype(BF16)
    gb_ref[...] = _sigmoid(_dot(xb, w_ref[:, 2 * D_MODEL:])).astype(BF16)


def _decay_kernel(x_ref, wt_ref, alog_ref, dtb_ref, beta_ref, g_ref):
    ba = lax.dot_general(wt_ref[...], x_ref[...], (((1,), (1,)), ((), ())),
                         precision=HIGHEST, preferred_element_type=F32)
    b, a = ba[:GDN_HEADS], ba[GDN_HEADS:]
    beta_ref[...] = _sigmoid(b)
    s = a + dtb_ref[...]
    softplus = jnp.maximum(s, 0.0) + jnp.log1p(jnp.exp(-jnp.abs(s)))
    g = -jnp.exp(alog_ref[...]) * softplus
    tm = g.shape[1]
    r = lax.broadcasted_iota(I32, (tm, tm), 0)
    c = lax.broadcasted_iota(I32, (tm, tm), 1)
    upper = ((r <= c) & (r // GDN_CHUNK == c // GDN_CHUNK)).astype(F32)
    g_ref[...] = jnp.dot(g, upper, precision=HIGHEST, preferred_element_type=F32)


def _token_spec(tm, width):
    return pl.BlockSpec((tm, width), lambda i: (i, 0))


def _const_spec(shape):
    return pl.BlockSpec(shape, lambda *_: (0,) * len(shape))


def _input_projections(xf, w_in, rope, a_log, dt_bias, B, S):
    T = xf.shape[0]
    tm = min(TOKEN_TILE, S)
    grid = (T // tm,)
    d = D_MODEL
    w_diff = w_in[:, :2 * d].astype(BF16)
    w_vt = w_in[:, 2 * d:3 * d].T.astype(BF16)
    w_gdn = w_in[:, 3 * d:6 * d].astype(BF16)
    w_z = w_in[:, 6 * d:7 * d]
    w_ba_t = w_in[:, 7 * d:7 * d + 2 * GDN_HEADS].T
    w_gate = jnp.concatenate([w_z, w_in[:, 7 * d + 2 * GDN_HEADS:]], 1).astype(BF16)
    x_spec = _token_spec(tm, d)
    act = jax.ShapeDtypeStruct((T, d), BF16)

    tk = min(ATTN_Q_TILE, S)
    nt = S // tm
    dq, dk, dvt = pl.pallas_call(
        _proj_diff_kernel, grid=grid,
        in_specs=[x_spec, _const_spec((d, 2 * d)), _const_spec((d, d))] + [_token_spec(tm, LANES)] * 3,
        out_specs=[_token_spec(tm, d), _token_spec(tm, d),
                   pl.BlockSpec((None, DIFF_HEADS, tm // tk, LANES, tk),
                                lambda i: (i // nt, 0, i % nt, 0, 0))],
        out_shape=[act, act, jax.ShapeDtypeStruct((B, DIFF_HEADS, S // tk, LANES, tk), BF16)],
        compiler_params=_params("arbitrary"),
    )(xf, w_diff, w_vt, *rope)

    gdn_raw = pl.pallas_call(
        _proj_plain_kernel, grid=grid,
        in_specs=[x_spec, _const_spec((d, 3 * d))],
        out_specs=_token_spec(tm, 3 * d),
        out_shape=jax.ShapeDtypeStruct((T, 3 * d), BF16),
        compiler_params=_params("arbitrary"),
    )(xf, w_gdn)

    sz, sga, sgb = pl.pallas_call(
        _proj_gate_kernel, grid=grid,
        in_specs=[x_spec, _const_spec((d, 3 * d))],
        out_specs=[_token_spec(tm, d)] * 3, out_shape=[act] * 3,
        compiler_params=_params("arbitrary"),
    )(xf, w_gate)

    td = min(256, T)
    lane_spec = pl.BlockSpec((GDN_HEADS, td), lambda i: (0, i))
    beta_t, g_t = pl.pallas_call(
        _decay_kernel, grid=(T // td,),
        in_specs=[_token_spec(td, d), _const_spec((2 * GDN_HEADS, d)),
                  _const_spec((GDN_HEADS, 1)), _const_spec((GDN_HEADS, 1))],
        out_specs=[lane_spec] * 2,
        out_shape=[jax.ShapeDtypeStruct((GDN_HEADS, T), F32)] * 2,
        compiler_params=_params("arbitrary"),
    )(xf, w_ba_t, a_log.reshape(GDN_HEADS, 1), dt_bias.reshape(GDN_HEADS, 1))
    return dq, dk, dvt, gdn_raw, sz, sga, sgb, beta_t, g_t


def _diff_attn_kernel(lam_ref, g_ref, q_ref, k_ref, vt_ref, o_ref,
                      sa_ref, sb_ref, m_ref, l_ref, acc_ref, *, lambda_init):
    tq = q_ref.shape[0]
    qi = pl.program_id(2)
    q = q_ref[...]
    lane = lax.broadcasted_iota(I32, q.shape, 1)
    zero = jnp.zeros_like(q)
    q2 = jnp.concatenate([jnp.where(lane < DIFF_HEAD_DIM, q, zero),
                          jnp.where(lane >= DIFF_HEAD_DIM, q, zero)], 0)

    def scores(j):
        return _dot_nt(k_ref[pl.ds(pl.multiple_of(j * tq, tq), tq), :], q2)

    def update(s, j):
        m = m_ref[...]
        m_new = jnp.maximum(m, jnp.max(s, 0, keepdims=True))
        alpha = jnp.exp2(m - m_new)
        p = jnp.exp2(s - m_new)
        l_ref[...] = alpha * l_ref[...] + jnp.sum(p, 0, keepdims=True)
        acc_ref[...] = alpha * acc_ref[...] + _dot(vt_ref[j], p.astype(BF16))
        m_ref[...] = m_new

    m_ref[...] = jnp.full(m_ref.shape, -jnp.inf, F32)
    l_ref[...] = jnp.zeros_like(l_ref)
    acc_ref[...] = jnp.zeros_like(acc_ref)
    sa_ref[...] = scores(0)

    def pair(i, _):
        j = 2 * i
        sb_ref[...] = scores(j + 1)
        update(sa_ref[...], j)
        sa_ref[...] = scores(j + 2)
        update(sb_ref[...], j + 1)
        return 0

    lax.fori_loop(0, qi // 2, pair, 0)
    key = lax.broadcasted_iota(I32, (tq, 2 * tq), 0)
    qry = lax.broadcasted_iota(I32, (tq, 2 * tq), 1)
    causal = key <= jnp.where(qry >= tq, qry - tq, qry)

    @pl.when(qi % 2 == 1)
    def _():
        sb_ref[...] = scores(qi)
        update(sa_ref[...], qi - 1)
        update(jnp.where(causal, sb_ref[...], -jnp.inf), qi)

    @pl.when(qi % 2 == 0)
    def _():
        update(jnp.where(causal, sa_ref[...], -jnp.inf), qi)

    o = acc_ref[...] / l_ref[...]
    lv = lam_ref[...]
    lam = (jnp.exp(jnp.sum(lv[0:1] * lv[1:2], -1, keepdims=True))
           - jnp.exp(jnp.sum(lv[2:3] * lv[3:4], -1, keepdims=True)) + lambda_init)
    o = o[:, :tq] - lam * o[:, tq:]
    o = o * lax.rsqrt(jnp.mean(o * o, 0, keepdims=True) + 1e-6) * g_ref[...]
    o_ref[...] = (o * (1.0 - lambda_init)).T.astype(BF16)


def _diff_attention(dq, dk, dvt, lam_vecs, subln_g, B, S, lambda_init):
    tq = min(ATTN_Q_TILE, S)
    d = D_MODEL
    q3, k3 = dq.reshape(B, S, d), dk.reshape(B, S, d)
    out = pl.pallas_call(
        functools.partial(_diff_attn_kernel, lambda_init=lambda_init),
        grid=(B, DIFF_HEADS, S // tq),
        in_specs=[_const_spec((4, DIFF_HEAD_DIM)), _const_spec((LANES, 1)),
                  pl.BlockSpec((None, tq, LANES), lambda b, h, i: (b, i, h)),
                  pl.BlockSpec((None, S, LANES), lambda b, h, i: (b, 0, h)),
                  pl.BlockSpec((None, None, S // tq, LANES, tq), lambda b, h, i: (b, h, 0, 0, 0))],
        out_specs=pl.BlockSpec((None, tq, LANES), lambda b, h, i: (b, i, h)),
        out_shape=jax.ShapeDtypeStruct((B, S, d), BF16),
        scratch_shapes=[pltpu.VMEM((tq, 2 * tq), F32), pltpu.VMEM((tq, 2 * tq), F32),
                        pltpu.VMEM((1, 2 * tq), F32), pltpu.VMEM((1, 2 * tq), F32),
                        pltpu.VMEM((LANES, 2 * tq), F32)],
        compiler_params=_params("arbitrary", "arbitrary", "arbitrary"),
    )(lam_vecs, subln_g.reshape(LANES, 1), q3, k3, dvt)
    return out.reshape(B * S, d)


def _gdn_prep_kernel(x_ref, cw_ref, o_ref, *, kind):
    x = x_ref[...].astype(F32)
    row = lax.broadcasted_iota(I32, x.shape, 0)
    y = x * cw_ref[CONV_WIDTH - 1:CONV_WIDTH, :]
    for j in range(1, CONV_WIDTH):
        shifted = jnp.where(row >= j, pltpu.roll(x, j, 0), 0.0)
        y = y + shifted * cw_ref[CONV_WIDTH - 1 - j:CONV_WIDTH - j, :]
    y = y * _sigmoid(y)
    if kind == "v":
        o_ref[...] = y.astype(BF16)
        return
    for h in range(x.shape[1] // GDN_HEAD_DIM):
        yh = y[:, h * GDN_HEAD_DIM:(h + 1) * GDN_HEAD_DIM]
        yh = yh * lax.rsqrt(jnp.sum(yh * yh, -1, keepdims=True) + 1e-6)
        if kind == "q":
            yh = yh * (GDN_HEAD_DIM ** -0.5)
        o_ref[:, h * GDN_HEAD_DIM:(h + 1) * GDN_HEAD_DIM] = yh.astype(BF16)


def _gdn_prep(gdn_raw, conv_w, B, S):
    d = D_MODEL
    cols = 2 * GDN_HEAD_DIM
    nblk = d // cols
    raw3 = gdn_raw.reshape(B, S, 3 * d)
    outs = []
    for idx, kind in enumerate(("q", "k", "v")):
        outs.append(pl.pallas_call(
            functools.partial(_gdn_prep_kernel, kind=kind),
            grid=(B, nblk),
            in_specs=[pl.BlockSpec((None, S, cols), lambda b, c, idx=idx: (b, 0, idx * nblk + c)),
                      pl.BlockSpec((CONV_WIDTH, cols), lambda b, c, idx=idx: (0, idx * nblk + c))],
            out_specs=pl.BlockSpec((None, S, cols), lambda b, c: (b, 0, c)),
            out_shape=jax.ShapeDtypeStruct((B, S, d), BF16),
            compiler_params=_params("arbitrary", "arbitrary"),
        )(raw3, conv_w))
    return outs


def _gdn_local_kernel(q_ref, k_ref, v_ref, col_ref, row_ref,
                      u_ref, w_ref, qd_ref, kd_ref, qk_ref, gl_ref):
    C = GDN_CHUNK
    R = q_ref.shape[0]
    ri = lax.broadcasted_iota(I32, (R, R), 0)
    ci = lax.broadcasted_iota(I32, (R, R), 1)
    same = (ri // C) == (ci // C)
    incl = same & (ri >= ci)
    strict = same & (ri > ci)
    eye = (ri == ci).astype(F32)
    heads = range(GDN_HEADS)
    head_lanes = [slice(h * GDN_HEAD_DIM, (h + 1) * GDN_HEAD_DIM) for h in heads]
    decays, pbs, t_mats = [], [], []
    for h in heads:
        kh = k_ref[:, head_lanes[h]]
        g_c = col_ref[:, GDN_HEADS + h:GDN_HEADS + h + 1]
        g_r = row_ref[GDN_HEADS + h:GDN_HEADS + h + 1, :]
        decay = jnp.where(incl, jnp.exp(jnp.where(incl, g_c - g_r, 0.0)), 0.0)
        p = -(jnp.where(strict, _dot_nt(kh, kh) * decay, 0.0) * col_ref[:, h:h + 1])
        decays.append(decay)
        pbs.append(p.astype(BF16))
        t_mats.append(eye + p)
    for _ in range(5):
        pbs = [_dot(pb, pb).astype(BF16) for pb in pbs]
        t_mats = [t + _dot(t.astype(BF16), pb) for t, pb in zip(t_mats, pbs)]
    for h in heads:
        lanes = head_lanes[h]
        qh, kh, vh = q_ref[:, lanes], k_ref[:, lanes], v_ref[:, lanes]
        beta_c = col_ref[:, h:h + 1]
        g_c = col_ref[:, GDN_HEADS + h:GDN_HEADS + h + 1]
        decay = decays[h]
        eg_c = jnp.exp(g_c)
        kf = kh.astype(F32)
        rhs = jnp.concatenate([vh.astype(F32) * beta_c, kf * (beta_c * eg_c)], 1)
        uw = _dot(t_mats[h].astype(BF16), rhs.astype(BF16))
        u_ref[:, lanes] = uw[:, :GDN_HEAD_DIM].astype(BF16)
        w_ref[:, lanes] = uw[:, GDN_HEAD_DIM:].astype(BF16)
        qd_ref[:, lanes] = (qh.astype(F32) * eg_c).astype(BF16)
        for c in range(R // C):
            rows = slice(c * C, (c + 1) * C)
            g_last = g_c[(c + 1) * C - 1:(c + 1) * C, :]
            kd_ref[rows, lanes] = (kf[rows] * jnp.exp(g_last - g_c[rows])).astype(BF16)
            gl_ref[c, h:h + 1, :] = jnp.broadcast_to(jnp.exp(g_last), (1, GDN_HEAD_DIM))
            qk_ref[c, h] = (_dot_nt(qh[rows], kh[rows]) * decay[rows, rows]).astype(BF16)


def _gdn_scan_kernel(u_ref, w_ref, qd_ref, kd_ref, qk_ref, gl_ref, sz_ref, ng_ref, o_ref, state_ref):
    @pl.when(pl.program_id(1) == 0)
    def _():
        state_ref[...] = jnp.zeros_like(state_ref)

    C = GDN_CHUNK
    ng = ng_ref[...]
    heads = range(GDN_HEADS)
    lanes = [slice(h * GDN_HEAD_DIM, (h + 1) * GDN_HEAD_DIM) for h in heads]
    for c in range(u_ref.shape[0] // C):
        rows = slice(c * C, (c + 1) * C)
        st = [state_ref[h] for h in heads]
        ws = [_dot(jnp.concatenate([w_ref[rows, lanes[h]], qd_ref[rows, lanes[h]]], 0),
                   st[h].astype(BF16)) for h in heads]
        vnb = [(u_ref[rows, lanes[h]].astype(F32) - ws[h][:C]).astype(BF16) for h in heads]
        for h in heads:
            state_ref[h] = st[h] * gl_ref[c, h:h + 1, :] + _dot_tn(kd_ref[rows, lanes[h]], vnb[h])
        for h in heads:
            o = ws[h][C:] + _dot(qk_ref[c, h], vnb[h])
            o = o * lax.rsqrt(jnp.mean(o * o, -1, keepdims=True) + 1e-6) * ng
            o_ref[rows, lanes[h]] = (o * sz_ref[rows, lanes[h]].astype(F32)).astype(BF16)


def _gated_deltanet(gq, gk, gv, sz, beta_t, g_t, norm_g, B, S):
    d = D_MODEL
    C = GDN_CHUNK
    R = min(GDN_GROUP * C, S)
    nc = R // C
    row_arr = jnp.concatenate([beta_t, g_t], 0).reshape(2 * GDN_HEADS, B, S).transpose(1, 0, 2)
    col_arr = row_arr.transpose(0, 2, 1)
    blk = pl.BlockSpec((None, R, d), lambda b, i: (b, i, 0))
    qk_blk = pl.BlockSpec((None, nc, GDN_HEADS, C, C), lambda b, i: (b, i, 0, 0, 0))
    gl_blk = pl.BlockSpec((None, nc, GDN_HEADS, GDN_HEAD_DIM), lambda b, i: (b, i, 0, 0))
    act = jax.ShapeDtypeStruct((B, S, d), BF16)
    u, w, qd, kd, qk, gl = pl.pallas_call(
        _gdn_local_kernel,
        grid=(B, S // R),
        in_specs=[blk, blk, blk,
                  pl.BlockSpec((None, R, 2 * GDN_HEADS), lambda b, i: (b, i, 0)),
                  pl.BlockSpec((None, 2 * GDN_HEADS, R), lambda b, i: (b, 0, i))],
        out_specs=[blk, blk, blk, blk, qk_blk, gl_blk],
        out_shape=[act, act, act, act,
                   jax.ShapeDtypeStruct((B, S // C, GDN_HEADS, C, C), BF16),
                   jax.ShapeDtypeStruct((B, S // C, GDN_HEADS, GDN_HEAD_DIM), F32)],
        compiler_params=_params("arbitrary", "arbitrary"),
    )(gq, gk, gv, col_arr, row_arr)
    out = pl.pallas_call(
        _gdn_scan_kernel,
        grid=(B, S // R),
        in_specs=[blk, blk, blk, blk, qk_blk, gl_blk, blk, _const_spec((1, GDN_HEAD_DIM))],
        out_specs=blk,
        out_shape=act,
        scratch_shapes=[pltpu.VMEM((GDN_HEADS, GDN_HEAD_DIM, GDN_HEAD_DIM), F32)],
        compiler_params=_params("arbitrary", "arbitrary"),
    )(u, w, qd, kd, qk, gl, sz.reshape(B, S, d), norm_g.reshape(1, GDN_HEAD_DIM))
    return out.reshape(B * S, d)


def _mix_kernel(x_ref, od_ref, og_ref, ga_ref, gb_ref, wd_ref, wg_ref, wm_ref, g_ref, b_ref,
                o_ref, *, alpha):
    yd = _dot(od_ref[...], wd_ref[...])
    yg = _dot(og_ref[...], wg_ref[...])
    m = ga_ref[...].astype(F32) * yd + gb_ref[...].astype(F32) * yg
    r = alpha * x_ref[...] + _dot(m.astype(BF16), wm_ref[...])
    o_ref[...] = _layer_norm(r, g_ref[...], b_ref[...])


def _mix(xf, od, og, sga, sgb, w_diff_o, w_gdn_o, w_mix_o, ln_g, ln_b, alpha):
    T, d = xf.shape
    tm = min(TOKEN_TILE, T)
    tok = _token_spec(tm, d)
    wspec = _const_spec((d, d))
    vec = _const_spec((1, d))
    return pl.pallas_call(
        functools.partial(_mix_kernel, alpha=alpha), grid=(T // tm,),
        in_specs=[tok] * 5 + [wspec] * 3 + [vec] * 2,
        out_specs=tok, out_shape=jax.ShapeDtypeStruct((T, d), F32),
        compiler_params=_params("arbitrary"),
    )(xf, od, og, sga, sgb, w_diff_o.astype(BF16), w_gdn_o.astype(BF16), w_mix_o.astype(BF16),
      ln_g.reshape(1, d), ln_b.reshape(1, d))


def _cross_kernel(x_ref, k_ref, v_ref, wq_ref, wo_ref, g_ref, b_ref, wr_ref, br_ref,
                  o_ref, idx_ref, gate_ref, *, alpha):
    half = x_ref.shape[0] // 2
    for part in range(2):
        rows = slice(part * half, (part + 1) * half)
        x = x_ref[rows, :]
        q = (_dot(x.astype(BF16), wq_ref[...]) * (MEM_HEAD_DIM ** -0.5)).astype(BF16)
        outs = []
        for h in range(MEM_HEADS):
            lanes = slice(h * MEM_HEAD_DIM, (h + 1) * MEM_HEAD_DIM)
            s = _dot_nt(q[:, lanes], k_ref[:, lanes])
            p = jnp.exp(s - jnp.max(s, -1, keepdims=True))
            p = p / jnp.sum(p, -1, keepdims=True)
            outs.append(_dot(p.astype(BF16), v_ref[:, lanes]))
        o = jnp.concatenate(outs, 1).astype(BF16)
        r = alpha * x + _dot(o, wo_ref[...])
        x2 = _layer_norm(r, g_ref[...], b_ref[...])
        o_ref[rows, :] = x2

        logits = _dot(x2.astype(BF16), wr_ref[...]) + br_ref[...]
        lane = lax.broadcasted_iota(I32, logits.shape, 1)
        vals, idxs = [], []
        for _ in range(TOP_K):
            mx = jnp.max(logits, -1, keepdims=True)
            ix = jnp.min(jnp.where(logits == mx, lane, N_EXPERTS), -1, keepdims=True)
            vals.append(mx)
            idxs.append(ix)
            logits = jnp.where(lane == ix, -jnp.inf, logits)
        e = jnp.exp(jnp.concatenate(vals, 1) - vals[0])
        gate_ref[rows, :] = e / jnp.sum(e, -1, keepdims=True)
        idx_ref[rows, :] = jnp.concatenate(idxs, 1)


def _cross_attention(x1, mem, w_cq, w_ck, w_cv, w_co, ln_g, ln_b, w_router, b_router, B, S, alpha):
    T, d = x1.shape
    M = mem.shape[1]
    w_kv = jnp.concatenate([w_ck, w_cv], 1).astype(BF16)
    tmem = min(TOKEN_TILE, B * M)
    kv = pl.pallas_call(
        _proj_plain_kernel, grid=(B * M // tmem,),
        in_specs=[_token_spec(tmem, d), _const_spec((d, 2 * d))],
        out_specs=_token_spec(tmem, 2 * d),
        out_shape=jax.ShapeDtypeStruct((B * M, 2 * d), BF16),
        compiler_params=_params("arbitrary"),
    )(mem.reshape(B * M, d), w_kv).reshape(B, M, 2 * d)

    tm = min(TOKEN_TILE, S)
    nt = S // tm
    tok = pl.BlockSpec((tm, d), lambda b, i: (b * nt + i, 0))
    small = pl.BlockSpec((tm, TOP_K), lambda b, i: (b * nt + i, 0))
    return pl.pallas_call(
        functools.partial(_cross_kernel, alpha=alpha), grid=(B, nt),
        in_specs=[tok,
                  pl.BlockSpec((None, M, d), lambda b, i: (b, 0, 0)),
                  pl.BlockSpec((None, M, d), lambda b, i: (b, 0, 1)),
                  _const_spec((d, d)), _const_spec((d, d)), _const_spec((1, d)), _const_spec((1, d)),
                  _const_spec((d, N_EXPERTS)), _const_spec((1, N_EXPERTS))],
        out_specs=[tok, small, small],
        out_shape=[jax.ShapeDtypeStruct((T, d), F32), jax.ShapeDtypeStruct((T, TOP_K), I32),
                   jax.ShapeDtypeStruct((T, TOP_K), F32)],
        compiler_params=_params("arbitrary", "arbitrary"),
    )(x1, kv, kv, w_cq.astype(BF16), w_co.astype(BF16), ln_g.reshape(1, d), ln_b.reshape(1, d),
      w_router.astype(BF16), b_router.reshape(1, N_EXPERTS))


def _rank_kernel(idx_ref, rank_ref, cnt_ref, run_ref):
    @pl.when(pl.program_id(0) == 0)
    def _():
        run_ref[...] = jnp.zeros_like(run_ref)

    idx = idx_ref[...]
    tm = idx.shape[0]
    lane = lax.broadcasted_iota(I32, (tm, N_EXPERTS), 1)
    hits = [lane == idx[:, k:k + 1] for k in range(TOP_K)]
    onehot = sum(hit.astype(F32) for hit in hits)
    r = lax.broadcasted_iota(I32, (tm, tm), 0)
    c = lax.broadcasted_iota(I32, (tm, tm), 1)
    lower = (r > c).astype(BF16)
    before = _dot(lower, onehot.astype(BF16)) + run_ref[...]
    ranks = [jnp.sum(jnp.where(hit, before, 0.0), -1, keepdims=True) for hit in hits]
    rank_ref[...] = jnp.concatenate(ranks, 1).astype(I32)
    run_ref[...] = run_ref[...] + jnp.sum(onehot, 0, keepdims=True)
    cnt_ref[...] = run_ref[...]


def _dest_kernel(idx_ref, rank_ref, start_ref, dest_ref):
    idx = idx_ref[...]
    lane = lax.broadcasted_iota(I32, (idx.shape[0], N_EXPERTS), 1)
    starts = start_ref[...]
    cols = [jnp.sum(jnp.where(lane == idx[:, k:k + 1], starts, 0), -1, keepdims=True)
            for k in range(TOP_K)]
    dest_ref[...] = rank_ref[...] + jnp.concatenate(cols, 1)


def _row_copy(src_ref, src_row, dst_ref, dst_row, sem):
    return pltpu.make_async_copy(src_ref.at[pl.ds(src_row, 1), :], dst_ref.at[pl.ds(dst_row, 1), :], sem)


def _index_copy(dest_hbm, dest_smem, sem_idx, step, slot, n):
    return pltpu.make_async_copy(dest_hbm.at[pl.ds(pl.multiple_of(step * n, n), n)],
                                 dest_smem.at[pl.ds(pl.multiple_of(slot * n, n), n)], sem_idx.at[slot])


def _dispatch_kernel(dest_hbm, x_ref, zeros_hbm, xs_hbm, dest_smem, sem_idx, sem_rows):
    del zeros_hbm
    tm = x_ref.shape[0]
    n = tm * TOP_K
    i = pl.program_id(0)
    last = pl.num_programs(0) - 1
    slot = i % 2

    @pl.when(i == 0)
    def _():
        _index_copy(dest_hbm, dest_smem, sem_idx, 0, 0, n).start()

    _index_copy(dest_hbm, dest_smem, sem_idx, i, slot, n).wait()

    @pl.when(i < last)
    def _():
        _index_copy(dest_hbm, dest_smem, sem_idx, i + 1, 1 - slot, n).start()

    base = slot * n

    def issue(jj, _):
        for u in range(ROW_UNROLL):
            j = jj * ROW_UNROLL + u
            tok = jj * (ROW_UNROLL // TOP_K) + u // TOP_K
            _row_copy(x_ref, tok, xs_hbm, dest_smem[base + j], sem_rows).start(priority=u % 2)
        return 0

    lax.fori_loop(0, n // ROW_UNROLL, issue, 0)
    for _ in range(TOP_K):
        pltpu.make_async_copy(x_ref, xs_hbm.at[pl.ds(0, tm), :], sem_rows).wait()


def _expert_kernel(te_ref, tv_ref, xs_ref, w1_ref, b1_ref, w2_ref, b2_ref, ys_ref):
    i = pl.program_id(0)

    @pl.when(tv_ref[i] != 0)
    def _():
        ff = w2_ref.shape[0]
        h = _dot(xs_ref[...].astype(BF16), w1_ref[...]) + b1_ref[...]
        h_gate = jnp.minimum(h[:, :ff], SWIGLU_LIMIT)
        h_up = jnp.clip(h[:, ff:], -SWIGLU_LIMIT, SWIGLU_LIMIT)
        act = h_gate * _sigmoid(SWIGLU_ALPHA * h_gate) * (h_up + 1.0)
        ys_ref[...] = _dot(act.astype(BF16), w2_ref[...]) + b2_ref[...]

    @pl.when(tv_ref[i] == 0)
    def _():
        ys_ref[...] = jnp.zeros_like(ys_ref)


def _combine_kernel(dest_hbm, ys_hbm, x_ref, gate_ref, g_ref, b_ref, o_ref,
                    dest_smem, buf, sem_idx, sem_rows, *, alpha):
    tm = x_ref.shape[0]
    n = tm * TOP_K
    i = pl.program_id(0)
    last = pl.num_programs(0) - 1
    slot = i % 2

    def gather_tile(step, s):
        cp = _index_copy(dest_hbm, dest_smem, sem_idx, step, s, n)
        cp.start()
        cp.wait()
        base = s * n

        def issue(jj, _):
            for u in range(ROW_UNROLL):
                j = jj * ROW_UNROLL + u
                tok = jj * (ROW_UNROLL // TOP_K) + u // TOP_K
                _row_copy(ys_hbm, dest_smem[base + j], buf.at[s, u % TOP_K], tok,
                          sem_rows.at[s]).start(priority=u % 2)
            return 0

        lax.fori_loop(0, n // ROW_UNROLL, issue, 0)

    @pl.when(i == 0)
    def _():
        gather_tile(0, 0)

    @pl.when(i < last)
    def _():
        gather_tile(i + 1, 1 - slot)

    for k in range(TOP_K):
        pltpu.make_async_copy(ys_hbm.at[pl.ds(0, tm), :], buf.at[slot, k], sem_rows.at[slot]).wait()
    gates = gate_ref[...]
    y = sum(gates[:, k:k + 1] * buf[slot, k] for k in range(TOP_K))
    o_ref[...] = _layer_norm(alpha * x_ref[...] + y, g_ref[...], b_ref[...])


def _moe(x2, idx, gates, w1, b1, w2, b2, ln_g, ln_b, alpha):
    T, d = x2.shape
    ff = w2.shape[1]
    tk = T * TOP_K
    bm = EXPERT_TILE
    n_tiles = -(-tk // bm) + N_EXPERTS
    R = n_tiles * bm

    tr = min(TOKEN_TILE, T)
    small = lambda tm: pl.BlockSpec((tm, TOP_K), lambda i: (i, 0))
    rank, counts = pl.pallas_call(
        _rank_kernel, grid=(T // tr,),
        in_specs=[small(tr)], out_specs=[small(tr), _const_spec((1, N_EXPERTS))],
        out_shape=[jax.ShapeDtypeStruct((T, TOP_K), I32), jax.ShapeDtypeStruct((1, N_EXPERTS), F32)],
        scratch_shapes=[pltpu.VMEM((1, N_EXPERTS), F32)],
        compiler_params=_params("arbitrary"),
    )(idx)

    counts = counts.reshape(N_EXPERTS).astype(I32)
    padded = (counts + bm - 1) // bm * bm
    pad_ends = jnp.cumsum(padded)
    pad_starts = pad_ends - padded
    tile_row = jnp.arange(n_tiles, dtype=I32) * bm
    tile_e = jnp.minimum(jnp.sum(pad_ends[None, :] <= tile_row[:, None], 1), N_EXPERTS - 1).astype(I32)
    tile_valid = (tile_row < pad_ends[-1]).astype(I32)

    dest = pl.pallas_call(
        _dest_kernel, grid=(T // tr,),
        in_specs=[small(tr), small(tr), _const_spec((1, N_EXPERTS))], out_specs=small(tr),
        out_shape=jax.ShapeDtypeStruct((T, TOP_K), I32),
        compiler_params=_params("arbitrary"),
    )(idx, rank, pad_starts.reshape(1, N_EXPERTS)).reshape(tk)

    tm = min(ROUTE_TILE, T)
    n = tm * TOP_K
    any_spec = pl.BlockSpec(memory_space=pl.ANY)
    xs = pl.pallas_call(
        _dispatch_kernel, grid=(T // tm,),
        in_specs=[any_spec, _token_spec(tm, d), any_spec], out_specs=any_spec,
        out_shape=jax.ShapeDtypeStruct((R, d), F32),
        scratch_shapes=[pltpu.SMEM((2 * n,), I32), pltpu.SemaphoreType.DMA((2,)), pltpu.SemaphoreType.DMA(())],
        input_output_aliases={2: 0},
        compiler_params=_params("arbitrary"),
    )(dest, x2, jnp.zeros((R, d), F32))

    ys = pl.pallas_call(
        _expert_kernel,
        grid_spec=pltpu.PrefetchScalarGridSpec(
            num_scalar_prefetch=2, grid=(n_tiles,),
            in_specs=[pl.BlockSpec((bm, d), lambda i, te, tv: (i, 0)),
                      pl.BlockSpec((None, d, 2 * ff), lambda i, te, tv: (te[i], 0, 0)),
                      pl.BlockSpec((None, 1, 2 * ff), lambda i, te, tv: (te[i], 0, 0)),
                      pl.BlockSpec((None, ff, d), lambda i, te, tv: (te[i], 0, 0)),
                      pl.BlockSpec((None, 1, d), lambda i, te, tv: (te[i], 0, 0))],
            out_specs=pl.BlockSpec((bm, d), lambda i, te, tv: (i, 0))),
        out_shape=jax.ShapeDtypeStruct((R, d), F32),
        compiler_params=_params("arbitrary"),
    )(tile_e, tile_valid, xs, w1.astype(BF16), b1.reshape(N_EXPERTS, 1, 2 * ff),
      w2.astype(BF16), b2.reshape(N_EXPERTS, 1, d))

    return pl.pallas_call(
        functools.partial(_combine_kernel, alpha=alpha), grid=(T // tm,),
        in_specs=[any_spec, any_spec, _token_spec(tm, d), small(tm), _const_spec((1, d)), _const_spec((1, d))],
        out_specs=_token_spec(tm, d),
        out_shape=jax.ShapeDtypeStruct((T, d), F32),
        scratch_shapes=[pltpu.SMEM((2 * n,), I32), pltpu.VMEM((2, TOP_K, tm, d), F32),
                        pltpu.SemaphoreType.DMA((2,)), pltpu.SemaphoreType.DMA((2,))],
        compiler_params=_params("arbitrary"),
    )(dest, ys, x2, gates, ln_g.reshape(1, d), ln_b.reshape(1, d))


def kernel(x, mem, positions, w_in, diff_lambda_q1, diff_lambda_k1, diff_lambda_q2, diff_lambda_k2, diff_subln_g, w_diff_o, gdn_conv_w, gdn_A_log, gdn_dt_bias, gdn_norm_g, w_gdn_o, w_mix_o, ln1_g, ln1_b, w_cq, w_ck, w_cv, w_co, ln2_g, ln2_b, w_router, b_router, w_exp_in, b_exp_in, w_exp_out, b_exp_out, ln3_g, ln3_b):
    B, S, d = x.shape
    depth = w_in.shape[0]
    alpha = (2 * depth) ** 0.25
    rope = _rope_tables(positions)
    xf = x.reshape(B * S, d)
    for l in range(depth):
        lambda_init = 0.8 - 0.6 * math.exp(-0.3 * l)
        dq, dk, dvt, gdn_raw, sz, sga, sgb, beta_t, g_t = _input_projections(
            xf, w_in[l], rope, gdn_A_log[l], gdn_dt_bias[l], B, S)
        lam_vecs = jnp.stack([diff_lambda_q1[l], diff_lambda_k1[l], diff_lambda_q2[l], diff_lambda_k2[l]])
        od = _diff_attention(dq, dk, dvt, lam_vecs, diff_subln_g[l], B, S, lambda_init)
        gq, gk, gv = _gdn_prep(gdn_raw, gdn_conv_w[l], B, S)
        og = _gated_deltanet(gq, gk, gv, sz, beta_t, g_t, gdn_norm_g[l], B, S)
        x1 = _mix(xf, od, og, sga, sgb, w_diff_o[l], w_gdn_o[l], w_mix_o[l], ln1_g[l], ln1_b[l], alpha)
        x2, idx, gates = _cross_attention(x1, mem, w_cq[l], w_ck[l], w_cv[l], w_co[l], ln2_g[l], ln2_b[l],
                                          w_router[l], b_router[l], B, S, alpha)
        xf = _moe(x2, idx, gates, w_exp_in[l], b_exp_in[l], w_exp_out[l], b_exp_out[l],
                  ln3_g[l], ln3_b[l], alpha)
    return xf.reshape(B, S, d)
```

```python
import functools
import math

import jax
import jax.numpy as jnp
from jax import lax
from jax.experimental import pallas as pl
from jax.experimental.pallas import tpu as pltpu

F32 = jnp.float32
BF16 = jnp.bfloat16
I32 = jnp.int32

D_MODEL = 1024
DIFF_HEADS = 8
DIFF_HEAD_DIM = 64
ROPE_THETA = 500000.0
ROT_DIM = DIFF_HEAD_DIM // 4
ROT_HALF = ROT_DIM // 2
GDN_HEADS = 8
GDN_HEAD_DIM = 128
CONV_WIDTH = 4
GDN_CHUNK = 64
MEM_HEADS = 4
MEM_HEAD_DIM = D_MODEL // MEM_HEADS
N_EXPERTS = 32
TOP_K = 4
SWIGLU_LIMIT = 7.0
SWIGLU_ALPHA = 1.702
LANES = 128
VMEM_LIMIT = 48 * 1024 * 1024

TOKEN_TILE = 512
ATTN_Q_TILE = 256
GDN_GROUP = 4
ROUTE_TILE = 256
EXPERT_TILE = 512
ROW_UNROLL = 8
HIGHEST = lax.Precision.HIGHEST


def _params(*sem):
    return pltpu.CompilerParams(dimension_semantics=sem, vmem_limit_bytes=VMEM_LIMIT)


def _dot(a, b):
    return jnp.dot(a, b, preferred_element_type=F32)


def _dot_nt(a, b):
    return lax.dot_general(a, b, (((1,), (1,)), ((), ())), preferred_element_type=F32)


def _dot_tn(a, b):
    return lax.dot_general(a, b, (((0,), (0,)), ((), ())), preferred_element_type=F32)


def _sigmoid(x):
    return 1.0 / (1.0 + jnp.exp(-x))


def _layer_norm(r, g, b):
    mu = jnp.mean(r, -1, keepdims=True)
    d = r - mu
    var = jnp.mean(d * d, -1, keepdims=True)
    return d * lax.rsqrt(var + 1e-5) * g + b


def _rope_kernel(pos_ref, inv_ref, cos_ref, sa_ref, sb_ref):
    ang = pos_ref[...] * inv_ref[...]
    sin = jnp.sin(ang)
    first_half = lax.broadcasted_iota(I32, ang.shape, 1) % DIFF_HEAD_DIM < ROT_HALF
    cos_ref[...] = jnp.cos(ang)
    sa_ref[...] = jnp.where(first_half, -sin, 0.0)
    sb_ref[...] = jnp.where(first_half, 0.0, sin)


def _rope_tables(positions):
    T = positions.size
    inv_freq = jnp.power(ROPE_THETA, -jnp.arange(0, ROT_DIM, 2, dtype=F32) / ROT_DIM)
    inv = jnp.concatenate([inv_freq, inv_freq, jnp.zeros((DIFF_HEAD_DIM - ROT_DIM,), F32)])
    inv = jnp.tile(inv, LANES // DIFF_HEAD_DIM).reshape(1, LANES)
    pos = jnp.broadcast_to(positions.reshape(T, 1).astype(F32), (T, LANES))
    rb = min(T, 1024)
    spec = pl.BlockSpec((rb, LANES), lambda i: (i, 0))
    return pl.pallas_call(
        _rope_kernel,
        grid=(T // rb,),
        in_specs=[spec, _const_spec((1, LANES))],
        out_specs=[spec] * 3,
        out_shape=[jax.ShapeDtypeStruct((T, LANES), F32)] * 3,
        compiler_params=_params("arbitrary"),
    )(pos, inv)


def _proj_diff_kernel(x_ref, w_ref, wvt_ref, cos_ref, sa_ref, sb_ref, q_ref, k_ref, vt_ref):
    xb = x_ref[...].astype(BF16)
    cos_t, sin_a, sin_b = cos_ref[...], sa_ref[...], sb_ref[...]
    qk_w = DIFF_HEADS * 2 * DIFF_HEAD_DIM
    for part, o_ref in enumerate((q_ref, k_ref)):
        acc = _dot(xb, w_ref[:, part * qk_w:(part + 1) * qk_w])
        for h in range(DIFF_HEADS):
            t = acc[:, h * LANES:(h + 1) * LANES]
            r = (t * cos_t + pltpu.roll(t, LANES - ROT_HALF, 1) * sin_a
                 + pltpu.roll(t, ROT_HALF, 1) * sin_b)
            if part == 0:
                r = r * (DIFF_HEAD_DIM ** -0.5 * math.log2(math.e))
            o_ref[:, h * LANES:(h + 1) * LANES] = r.astype(BF16)
    vt = _dot_nt(wvt_ref[...], xb)
    tk = vt_ref.shape[-1]
    for h in range(DIFF_HEADS):
        for c in range(vt_ref.shape[1]):
            vt_ref[h, c] = vt[h * LANES:(h + 1) * LANES, c * tk:(c + 1) * tk].astype(BF16)


def _proj_plain_kernel(x_ref, w_ref, o_ref):
    o_ref[...] = _dot(x_ref[...].astype(BF16), w_ref[...]).astype(BF16)


def _proj_gate_kernel(x_ref, w_ref, sz_ref, ga_ref, gb_ref):
    xb = x_ref[...].astype(BF16)
    z = _dot(xb, w_ref[:, :D_MODEL])
    sz_ref[...] = (z * _sigmoid(z)).astype(BF16)
    ga_ref[...] = _sigmoid(_dot(xb, w_ref[:, D_MODEL:2 * D_MODEL])).astype(BF16)
    gb_ref[...] = _sigmoid(_dot(xb, w_ref[:, 2 * D_MODEL:])).astype(BF16)


def _decay_kernel(x_ref, wt_ref, alog_ref, dtb_ref, beta_ref, g_ref):
    ba = lax.dot_general(wt_ref[...], x_ref[...], (((1,), (1,)), ((), ())),
                         precision=HIGHEST, preferred_element_type=F32)
    b, a = ba[:GDN_HEADS], ba[GDN_HEADS:]
    beta_ref[...] = _sigmoid(b)
    s = a + dtb_ref[...]
    softplus = jnp.maximum(s, 0.0) + jnp.log1p(jnp.exp(-jnp.abs(s)))
    g = -jnp.exp(alog_ref[...]) * softplus
    tm = g.shape[1]
    r = lax.broadcasted_iota(I32, (tm, tm), 0)
    c = lax.broadcasted_iota(I32, (tm, tm), 1)
    upper = ((r <= c) & (r // GDN_CHUNK == c // GDN_CHUNK)).astype(F32)
    g_ref[...] = jnp.dot(g, upper, precision=HIGHEST, preferred_element_type=F32)


def _token_spec(tm, width):
    return pl.BlockSpec((tm, width), lambda i: (i, 0))


def _const_spec(shape):
    return pl.BlockSpec(shape, lambda *_: (0,) * len(shape))


def _input_projections(xf, w_in, rope, a_log, dt_bias, B, S):
    T = xf.shape[0]
    tm = min(TOKEN_TILE, S)
    grid = (T // tm,)
    d = D_MODEL
    w_diff = w_in[:, :2 * d].astype(BF16)
    w_vt = w_in[:, 2 * d:3 * d].T.astype(BF16)
    w_gdn = w_in[:, 3 * d:6 * d].astype(BF16)
    w_z = w_in[:, 6 * d:7 * d]
    w_ba_t = w_in[:, 7 * d:7 * d + 2 * GDN_HEADS].T
    w_gate = jnp.concatenate([w_z, w_in[:, 7 * d + 2 * GDN_HEADS:]], 1).astype(BF16)
    x_spec = _token_spec(tm, d)
    act = jax.ShapeDtypeStruct((T, d), BF16)

    tk = min(ATTN_Q_TILE, S)
    nt = S // tm
    dq, dk, dvt = pl.pallas_call(
        _proj_diff_kernel, grid=grid,
        in_specs=[x_spec, _const_spec((d, 2 * d)), _const_spec((d, d))] + [_token_spec(tm, LANES)] * 3,
        out_specs=[_token_spec(tm, d), _token_spec(tm, d),
                   pl.BlockSpec((None, DIFF_HEADS, tm // tk, LANES, tk),
                                lambda i: (i // nt, 0, i % nt, 0, 0))],
        out_shape=[act, act, jax.ShapeDtypeStruct((B, DIFF_HEADS, S // tk, LANES, tk), BF16)],
        compiler_params=_params("arbitrary"),
    )(xf, w_diff, w_vt, *rope)

    gdn_raw = pl.pallas_call(
        _proj_plain_kernel, grid=grid,
        in_specs=[x_spec, _const_spec((d, 3 * d))],
        out_specs=_token_spec(tm, 3 * d),
        out_shape=jax.ShapeDtypeStruct((T, 3 * d), BF16),
        compiler_params=_params("arbitrary"),
    )(xf, w_gdn)

    sz, sga, sgb = pl.pallas_call(
        _proj_gate_kernel, grid=grid,
        in_specs=[x_spec, _const_spec((d, 3 * d))],
        out_specs=[_token_spec(tm, d)] * 3, out_shape=[act] * 3,
        compiler_params=_params("arbitrary"),
    )(xf, w_gate)

    td = min(256, T)
    lane_spec = pl.BlockSpec((GDN_HEADS, td), lambda i: (0, i))
    beta_t, g_t = pl.pallas_call(
        _decay_kernel, grid=(T // td,),
        in_specs=[_token_spec(td, d), _const_spec((2 * GDN_HEADS, d)),
                  _const_spec((GDN_HEADS, 1)), _const_spec((GDN_HEADS, 1))],
        out_specs=[lane_spec] * 2,
        out_shape=[jax.ShapeDtypeStruct((GDN_HEADS, T), F32)] * 2,
        compiler_params=_params("arbitrary"),
    )(xf, w_ba_t, a_log.reshape(GDN_HEADS, 1), dt_bias.reshape(GDN_HEADS, 1))
    return dq, dk, dvt, gdn_raw, sz, sga, sgb, beta_t, g_t


def _diff_attn_kernel(lam_ref, g_ref, q_ref, k_ref, vt_ref, o_ref,
                      sa_ref, sb_ref, m_ref, l_ref, acc_ref, *, lambda_init, tq):
    lv = lam_ref[...]
    lam = (jnp.exp(jnp.sum(lv[0:1] * lv[1:2], -1, keepdims=True))
           - jnp.exp(jnp.sum(lv[2:3] * lv[3:4], -1, keepdims=True)) + lambda_init)
    lane = lax.broadcasted_iota(I32, (tq, LANES), 1)
    key = lax.broadcasted_iota(I32, (tq, 2 * tq), 0)
    qry = lax.broadcasted_iota(I32, (tq, 2 * tq), 1)
    causal = key <= jnp.where(qry >= tq, qry - tq, qry)

    for qi in range(q_ref.shape[0] // tq):
        q = q_ref[qi * tq:(qi + 1) * tq, :]
        zero = jnp.zeros_like(q)
        q2 = jnp.concatenate([jnp.where(lane < DIFF_HEAD_DIM, q, zero),
                              jnp.where(lane >= DIFF_HEAD_DIM, q, zero)], 0)

        def scores(j, q2=q2):
            return _dot_nt(k_ref[pl.ds(pl.multiple_of(j * tq, tq), tq), :], q2)

        def update(s, j):
            m = m_ref[...]
            m_new = jnp.maximum(m, jnp.max(s, 0, keepdims=True))
            alpha = jnp.exp2(m - m_new)
            p = jnp.exp2(s - m_new)
            l_ref[...] = alpha * l_ref[...] + jnp.sum(p, 0, keepdims=True)
            acc_ref[...] = alpha * acc_ref[...] + _dot(vt_ref[j], p.astype(BF16))
            m_ref[...] = m_new

        m_ref[...] = jnp.full(m_ref.shape, -jnp.inf, F32)
        l_ref[...] = jnp.zeros_like(l_ref)
        acc_ref[...] = jnp.zeros_like(acc_ref)
        sa_ref[...] = scores(0)

        def pair(i, _, scores=scores, update=update):
            j = 2 * i
            sb_ref[...] = scores(j + 1)
            update(sa_ref[...], j)
            sa_ref[...] = scores(j + 2)
            update(sb_ref[...], j + 1)
            return 0

        if qi >= 2:
            lax.fori_loop(0, qi // 2, pair, 0)
        if qi % 2 == 1:
            sb_ref[...] = scores(qi)
            update(sa_ref[...], qi - 1)
            update(jnp.where(causal, sb_ref[...], -jnp.inf), qi)
        else:
            update(jnp.where(causal, sa_ref[...], -jnp.inf), qi)

        o = acc_ref[...] / l_ref[...]
        o = o[:, :tq] - lam * o[:, tq:]
        o = o * lax.rsqrt(jnp.mean(o * o, 0, keepdims=True) + 1e-6) * g_ref[...]
        o_ref[qi * tq:(qi + 1) * tq, :] = (o * (1.0 - lambda_init)).T.astype(BF16)


def _diff_attention(dq, dk, dvt, lam_vecs, subln_g, B, S, lambda_init):
    tq = min(ATTN_Q_TILE, S)
    d = D_MODEL
    q3, k3 = dq.reshape(B, S, d), dk.reshape(B, S, d)
    seq = pl.BlockSpec((None, S, LANES), lambda b, h: (b, 0, h))
    out = pl.pallas_call(
        functools.partial(_diff_attn_kernel, lambda_init=lambda_init, tq=tq),
        grid=(B, DIFF_HEADS),
        in_specs=[_const_spec((4, DIFF_HEAD_DIM)), _const_spec((LANES, 1)), seq, seq,
                  pl.BlockSpec((None, None, S // tq, LANES, tq), lambda b, h: (b, h, 0, 0, 0))],
        out_specs=seq,
        out_shape=jax.ShapeDtypeStruct((B, S, d), BF16),
        scratch_shapes=[pltpu.VMEM((tq, 2 * tq), F32), pltpu.VMEM((tq, 2 * tq), F32),
                        pltpu.VMEM((1, 2 * tq), F32), pltpu.VMEM((1, 2 * tq), F32),
                        pltpu.VMEM((LANES, 2 * tq), F32)],
        compiler_params=_params("arbitrary", "arbitrary"),
    )(lam_vecs, subln_g.reshape(LANES, 1), q3, k3, dvt)
    return out.reshape(B * S, d)


def _gdn_prep_kernel(x_ref, cw_ref, o_ref, *, kind):
    x = x_ref[...].astype(F32)
    row = lax.broadcasted_iota(I32, x.shape, 0)
    y = x * cw_ref[CONV_WIDTH - 1:CONV_WIDTH, :]
    for j in range(1, CONV_WIDTH):
        shifted = jnp.where(row >= j, pltpu.roll(x, j, 0), 0.0)
        y = y + shifted * cw_ref[CONV_WIDTH - 1 - j:CONV_WIDTH - j, :]
    y = y * _sigmoid(y)
    if kind == "v":
        o_ref[...] = y.astype(BF16)
        return
    for h in range(x.shape[1] // GDN_HEAD_DIM):
        yh = y[:, h * GDN_HEAD_DIM:(h + 1) * GDN_HEAD_DIM]
        yh = yh * lax.rsqrt(jnp.sum(yh * yh, -1, keepdims=True) + 1e-6)
        if kind == "q":
            yh = yh * (GDN_HEAD_DIM ** -0.5)
        o_ref[:, h * GDN_HEAD_DIM:(h + 1) * GDN_HEAD_DIM] = yh.astype(BF16)


def _gdn_prep(gdn_raw, conv_w, B, S):
    d = D_MODEL
    cols = 2 * GDN_HEAD_DIM
    nblk = d // cols
    raw3 = gdn_raw.reshape(B, S, 3 * d)
    outs = []
    for idx, kind in enumerate(("q", "k", "v")):
        outs.append(pl.pallas_call(
            functools.partial(_gdn_prep_kernel, kind=kind),
            grid=(B, nblk),
            in_specs=[pl.BlockSpec((None, S, cols), lambda b, c, idx=idx: (b, 0, idx * nblk + c)),
                      pl.BlockSpec((CONV_WIDTH, cols), lambda b, c, idx=idx: (0, idx * nblk + c))],
            out_specs=pl.BlockSpec((None, S, cols), lambda b, c: (b, 0, c)),
            out_shape=jax.ShapeDtypeStruct((B, S, d), BF16),
            compiler_params=_params("arbitrary", "arbitrary"),
        )(raw3, conv_w))
    return outs


def _gdn_local_kernel(q_ref, k_ref, v_ref, col_ref, row_ref,
                      u_ref, w_ref, qd_ref, kd_ref, qk_ref, gl_ref):
    C = GDN_CHUNK
    R = q_ref.shape[0]
    ri = lax.broadcasted_iota(I32, (R, R), 0)
    ci = lax.broadcasted_iota(I32, (R, R), 1)
    same = (ri // C) == (ci // C)
    incl = same & (ri >= ci)
    strict = same & (ri > ci)
    eye = (ri == ci).astype(F32)
    heads = range(GDN_HEADS)
    head_lanes = [slice(h * GDN_HEAD_DIM, (h + 1) * GDN_HEAD_DIM) for h in heads]
    decays, pbs, t_mats = [], [], []
    for h in heads:
        kh = k_ref[:, head_lanes[h]]
        g_c = col_ref[:, GDN_HEADS + h:GDN_HEADS + h + 1]
        g_r = row_ref[GDN_HEADS + h:GDN_HEADS + h + 1, :]
        decay = jnp.where(incl, jnp.exp(jnp.where(incl, g_c - g_r, 0.0)), 0.0)
        p = -(jnp.where(strict, _dot_nt(kh, kh) * decay, 0.0) * col_ref[:, h:h + 1])
        decays.append(decay)
        pbs.append(p.astype(BF16))
        t_mats.append(eye + p)
    for _ in range(5):
        pbs = [_dot(pb, pb).astype(BF16) for pb in pbs]
        t_mats = [t + _dot(t.astype(BF16), pb) for t, pb in zip(t_mats, pbs)]
    for h in heads:
        lanes = head_lanes[h]
        qh, kh, vh = q_ref[:, lanes], k_ref[:, lanes], v_ref[:, lanes]
        beta_c = col_ref[:, h:h + 1]
        g_c = col_ref[:, GDN_HEADS + h:GDN_HEADS + h + 1]
        decay = decays[h]
        eg_c = jnp.exp(g_c)
        kf = kh.astype(F32)
        rhs = jnp.concatenate([vh.astype(F32) * beta_c, kf * (beta_c * eg_c)], 1)
        uw = _dot(t_mats[h].astype(BF16), rhs.astype(BF16))
        u_ref[:, lanes] = uw[:, :GDN_HEAD_DIM].astype(BF16)
        w_ref[:, lanes] = uw[:, GDN_HEAD_DIM:].astype(BF16)
        qd_ref[:, lanes] = (qh.astype(F32) * eg_c).astype(BF16)
        for c in range(R // C):
            rows = slice(c * C, (c + 1) * C)
            g_last = g_c[(c + 1) * C - 1:(c + 1) * C, :]
            kd_ref[rows, lanes] = (kf[rows] * jnp.exp(g_last - g_c[rows])).astype(BF16)
            gl_ref[c, h:h + 1, :] = jnp.broadcast_to(jnp.exp(g_last), (1, GDN_HEAD_DIM))
            qk_ref[c, h] = (_dot_nt(qh[rows], kh[rows]) * decay[rows, rows]).astype(BF16)


def _gdn_scan_kernel(u_ref, w_ref, qd_ref, kd_ref, qk_ref, gl_ref, sz_ref, ng_ref, o_ref, state_ref):
    @pl.when(pl.program_id(1) == 0)
    def _():
        state_ref[...] = jnp.zeros_like(state_ref)

    C = GDN_CHUNK
    ng = ng_ref[...]
    heads = range(GDN_HEADS)
    lanes = [slice(h * GDN_HEAD_DIM, (h + 1) * GDN_HEAD_DIM) for h in heads]
    for c in range(u_ref.shape[0] // C):
        rows = slice(c * C, (c + 1) * C)
        st = [state_ref[h] for h in heads]
        ws = [_dot(jnp.concatenate([w_ref[rows, lanes[h]], qd_ref[rows, lanes[h]]], 0),
                   st[h].astype(BF16)) for h in heads]
        vnb = [(u_ref[rows, lanes[h]].astype(F32) - ws[h][:C]).astype(BF16) for h in heads]
        for h in heads:
            state_ref[h] = st[h] * gl_ref[c, h:h + 1, :] + _dot_tn(kd_ref[rows, lanes[h]], vnb[h])
        for h in heads:
            o = ws[h][C:] + _dot(qk_ref[c, h], vnb[h])
            o = o * lax.rsqrt(jnp.mean(o * o, -1, keepdims=True) + 1e-6) * ng
            o_ref[rows, lanes[h]] = (o * sz_ref[rows, lanes[h]].astype(F32)).astype(BF16)


def _gated_deltanet(gq, gk, gv, sz, beta_t, g_t, norm_g, B, S):
    d = D_MODEL
    C = GDN_CHUNK
    R = min(GDN_GROUP * C, S)
    nc = R // C
    row_arr = jnp.concatenate([beta_t, g_t], 0).reshape(2 * GDN_HEADS, B, S).transpose(1, 0, 2)
    col_arr = row_arr.transpose(0, 2, 1)
    blk = pl.BlockSpec((None, R, d), lambda b, i: (b, i, 0))
    qk_blk = pl.BlockSpec((None, nc, GDN_HEADS, C, C), lambda b, i: (b, i, 0, 0, 0))
    gl_blk = pl.BlockSpec((None, nc, GDN_HEADS, GDN_HEAD_DIM), lambda b, i: (b, i, 0, 0))
    act = jax.ShapeDtypeStruct((B, S, d), BF16)
    u, w, qd, kd, qk, gl = pl.pallas_call(
        _gdn_local_kernel,
        grid=(B, S // R),
        in_specs=[blk, blk, blk,
                  pl.BlockSpec((None, R, 2 * GDN_HEADS), lambda b, i: (b, i, 0)),
                  pl.BlockSpec((None, 2 * GDN_HEADS, R), lambda b, i: (b, 0, i))],
        out_specs=[blk, blk, blk, blk, qk_blk, gl_blk],
        out_shape=[act, act, act, act,
                   jax.ShapeDtypeStruct((B, S // C, GDN_HEADS, C, C), BF16),
                   jax.ShapeDtypeStruct((B, S // C, GDN_HEADS, GDN_HEAD_DIM), F32)],
        compiler_params=_params("arbitrary", "arbitrary"),
    )(gq, gk, gv, col_arr, row_arr)
    out = pl.pallas_call(
        _gdn_scan_kernel,
        grid=(B, S // R),
        in_specs=[blk, blk, blk, blk, qk_blk, gl_blk, blk, _const_spec((1, GDN_HEAD_DIM))],
        out_specs=blk,
        out_shape=act,
        scratch_shapes=[pltpu.VMEM((GDN_HEADS, GDN_HEAD_DIM, GDN_HEAD_DIM), F32)],
        compiler_params=_params("arbitrary", "arbitrary"),
    )(u, w, qd, kd, qk, gl, sz.reshape(B, S, d), norm_g.reshape(1, GDN_HEAD_DIM))
    return out.reshape(B * S, d)


def _mix_kernel(x_ref, od_ref, og_ref, ga_ref, gb_ref, wd_ref, wg_ref, wm_ref, g_ref, b_ref,
                o_ref, *, alpha):
    yd = _dot(od_ref[...], wd_ref[...])
    yg = _dot(og_ref[...], wg_ref[...])
    m = ga_ref[...].astype(F32) * yd + gb_ref[...].astype(F32) * yg
    r = alpha * x_ref[...] + _dot(m.astype(BF16), wm_ref[...])
    o_ref[...] = _layer_norm(r, g_ref[...], b_ref[...])


def _mix(xf, od, og, sga, sgb, w_diff_o, w_gdn_o, w_mix_o, ln_g, ln_b, alpha):
    T, d = xf.shape
    tm = min(TOKEN_TILE, T)
    tok = _token_spec(tm, d)
    wspec = _const_spec((d, d))
    vec = _const_spec((1, d))
    return pl.pallas_call(
        functools.partial(_mix_kernel, alpha=alpha), grid=(T // tm,),
        in_specs=[tok] * 5 + [wspec] * 3 + [vec] * 2,
        out_specs=tok, out_shape=jax.ShapeDtypeStruct((T, d), F32),
        compiler_params=_params("arbitrary"),
    )(xf, od, og, sga, sgb, w_diff_o.astype(BF16), w_gdn_o.astype(BF16), w_mix_o.astype(BF16),
      ln_g.reshape(1, d), ln_b.reshape(1, d))


def _cross_kernel(x_ref, k_ref, v_ref, wq_ref, wo_ref, g_ref, b_ref, wr_ref, br_ref,
                  o_ref, idx_ref, gate_ref, *, alpha):
    half = x_ref.shape[0] // 2
    for part in range(2):
        rows = slice(part * half, (part + 1) * half)
        x = x_ref[rows, :]
        q = (_dot(x.astype(BF16), wq_ref[...]) * (MEM_HEAD_DIM ** -0.5)).astype(BF16)
        outs = []
        for h in range(MEM_HEADS):
            lanes = slice(h * MEM_HEAD_DIM, (h + 1) * MEM_HEAD_DIM)
            s = _dot_nt(q[:, lanes], k_ref[:, lanes])
            p = jnp.exp(s - jnp.max(s, -1, keepdims=True))
            p = p / jnp.sum(p, -1, keepdims=True)
            outs.append(_dot(p.astype(BF16), v_ref[:, lanes]))
        o = jnp.concatenate(outs, 1).astype(BF16)
        r = alpha * x + _dot(o, wo_ref[...])
        x2 = _layer_norm(r, g_ref[...], b_ref[...])
        o_ref[rows, :] = x2

        logits = _dot(x2.astype(BF16), wr_ref[...]) + br_ref[...]
        lane = lax.broadcasted_iota(I32, logits.shape, 1)
        vals, idxs = [], []
        for _ in range(TOP_K):
            mx = jnp.max(logits, -1, keepdims=True)
            ix = jnp.min(jnp.where(logits == mx, lane, N_EXPERTS), -1, keepdims=True)
            vals.append(mx)
            idxs.append(ix)
            logits = jnp.where(lane == ix, -jnp.inf, logits)
        e = jnp.exp(jnp.concatenate(vals, 1) - vals[0])
        gate_ref[rows, :] = e / jnp.sum(e, -1, keepdims=True)
        idx_ref[rows, :] = jnp.concatenate(idxs, 1)


def _cross_attention(x1, mem, w_cq, w_ck, w_cv, w_co, ln_g, ln_b, w_router, b_router, B, S, alpha):
    T, d = x1.shape
    M = mem.shape[1]
    w_kv = jnp.concatenate([w_ck, w_cv], 1).astype(BF16)
    tmem = min(TOKEN_TILE, B * M)
    kv = pl.pallas_call(
        _proj_plain_kernel, grid=(B * M // tmem,),
        in_specs=[_token_spec(tmem, d), _const_spec((d, 2 * d))],
        out_specs=_token_spec(tmem, 2 * d),
        out_shape=jax.ShapeDtypeStruct((B * M, 2 * d), BF16),
        compiler_params=_params("arbitrary"),
    )(mem.reshape(B * M, d), w_kv).reshape(B, M, 2 * d)

    tm = min(TOKEN_TILE, S)
    nt = S // tm
    tok = pl.BlockSpec((tm, d), lambda b, i: (b * nt + i, 0))
    small = pl.BlockSpec((tm, TOP_K), lambda b, i: (b * nt + i, 0))
    return pl.pallas_call(
        functools.partial(_cross_kernel, alpha=alpha), grid=(B, nt),
        in_specs=[tok,
                  pl.BlockSpec((None, M, d), lambda b, i: (b, 0, 0)),
                  pl.BlockSpec((None, M, d), lambda b, i: (b, 0, 1)),
                  _const_spec((d, d)), _const_spec((d, d)), _const_spec((1, d)), _const_spec((1, d)),
                  _const_spec((d, N_EXPERTS)), _const_spec((1, N_EXPERTS))],
        out_specs=[tok, small, small],
        out_shape=[jax.ShapeDtypeStruct((T, d), F32), jax.ShapeDtypeStruct((T, TOP_K), I32),
                   jax.ShapeDtypeStruct((T, TOP_K), F32)],
        compiler_params=_params("arbitrary", "arbitrary"),
    )(x1, kv, kv, w_cq.astype(BF16), w_co.astype(BF16), ln_g.reshape(1, d), ln_b.reshape(1, d),
      w_router.astype(BF16), b_router.reshape(1, N_EXPERTS))


def _rank_kernel(idx_ref, rank_ref, cnt_ref, run_ref):
    @pl.when(pl.program_id(0) == 0)
    def _():
        run_ref[...] = jnp.zeros_like(run_ref)

    idx = idx_ref[...]
    tm = idx.shape[0]
    lane = lax.broadcasted_iota(I32, (tm, N_EXPERTS), 1)
    hits = [lane == idx[:, k:k + 1] for k in range(TOP_K)]
    onehot = sum(hit.astype(F32) for hit in hits)
    r = lax.broadcasted_iota(I32, (tm, tm), 0)
    c = lax.broadcasted_iota(I32, (tm, tm), 1)
    lower = (r > c).astype(BF16)
    before = _dot(lower, onehot.astype(BF16)) + run_ref[...]
    ranks = [jnp.sum(jnp.where(hit, before, 0.0), -1, keepdims=True) for hit in hits]
    rank_ref[...] = jnp.concatenate(ranks, 1).astype(I32)
    run_ref[...] = run_ref[...] + jnp.sum(onehot, 0, keepdims=True)
    cnt_ref[...] = run_ref[...]


def _dest_kernel(idx_ref, rank_ref, start_ref, dest_ref):
    idx = idx_ref[...]
    lane = lax.broadcasted_iota(I32, (idx.shape[0], N_EXPERTS), 1)
    starts = start_ref[...]
    cols = [jnp.sum(jnp.where(lane == idx[:, k:k + 1], starts, 0), -1, keepdims=True)
            for k in range(TOP_K)]
    dest_ref[...] = rank_ref[...] + jnp.concatenate(cols, 1)


def _row_copy(src_ref, src_row, dst_ref, dst_row, sem):
    return pltpu.make_async_copy(src_ref.at[pl.ds(src_row, 1), :], dst_ref.at[pl.ds(dst_row, 1), :], sem)


def _index_copy(dest_hbm, dest_smem, sem_idx, step, slot, n):
    return pltpu.make_async_copy(dest_hbm.at[pl.ds(pl.multiple_of(step * n, n), n)],
                                 dest_smem.at[pl.ds(pl.multiple_of(slot * n, n), n)], sem_idx.at[slot])


def _dispatch_kernel(zero_rows, dest_hbm, x_ref, xs_hbm, dest_smem, zero_buf, sem_idx, sem_rows):
    tm = x_ref.shape[0]
    n = tm * TOP_K
    i = pl.program_id(0)
    last = pl.num_programs(0) - 1
    slot = i % 2

    @pl.when(i == 0)
    def _():
        _index_copy(dest_hbm, dest_smem, sem_idx, 0, 0, n).start()
        zero_buf[...] = jnp.zeros_like(zero_buf)
        bm = zero_buf.shape[0]

        def zero_tile(t):
            row = pl.multiple_of(jnp.maximum(zero_rows[t], 0), bm)
            return pltpu.make_async_copy(zero_buf, xs_hbm.at[pl.ds(row, bm), :], sem_rows)

        for t in range(zero_rows.shape[0]):
            pl.when(zero_rows[t] >= 0)(lambda t=t: zero_tile(t).start())
        for t in range(zero_rows.shape[0]):
            pl.when(zero_rows[t] >= 0)(lambda t=t: zero_tile(t).wait())

    _index_copy(dest_hbm, dest_smem, sem_idx, i, slot, n).wait()

    @pl.when(i < last)
    def _():
        _index_copy(dest_hbm, dest_smem, sem_idx, i + 1, 1 - slot, n).start()

    base = slot * n

    def issue(jj, _):
        for u in range(ROW_UNROLL):
            j = jj * ROW_UNROLL + u
            tok = jj * (ROW_UNROLL // TOP_K) + u // TOP_K
            _row_copy(x_ref, tok, xs_hbm, dest_smem[base + j], sem_rows).start(priority=u % 2)
        return 0

    lax.fori_loop(0, n // ROW_UNROLL, issue, 0)
    for _ in range(TOP_K):
        pltpu.make_async_copy(x_ref, xs_hbm.at[pl.ds(0, tm), :], sem_rows).wait()


def _expert_kernel(te_ref, tv_ref, xs_ref, w1_ref, b1_ref, w2_ref, b2_ref, ys_ref):
    i = pl.program_id(0)

    @pl.when(tv_ref[i] != 0)
    def _():
        ff = w2_ref.shape[0]
        h = _dot(xs_ref[...].astype(BF16), w1_ref[...]) + b1_ref[...]
        h_gate = jnp.minimum(h[:, :ff], SWIGLU_LIMIT)
        h_up = jnp.clip(h[:, ff:], -SWIGLU_LIMIT, SWIGLU_LIMIT)
        act = h_gate * _sigmoid(SWIGLU_ALPHA * h_gate) * (h_up + 1.0)
        ys_ref[...] = _dot(act.astype(BF16), w2_ref[...]) + b2_ref[...]

    @pl.when(tv_ref[i] == 0)
    def _():
        ys_ref[...] = jnp.zeros_like(ys_ref)


def _combine_kernel(dest_hbm, ys_hbm, x_ref, gate_ref, g_ref, b_ref, o_ref,
                    dest_smem, buf, sem_idx, sem_rows, *, alpha):
    tm = x_ref.shape[0]
    n = tm * TOP_K
    i = pl.program_id(0)
    last = pl.num_programs(0) - 1
    slot = i % 2

    def gather_tile(step, s):
        cp = _index_copy(dest_hbm, dest_smem, sem_idx, step, s, n)
        cp.start()
        cp.wait()
        base = s * n

        def issue(jj, _):
            for u in range(ROW_UNROLL):
                j = jj * ROW_UNROLL + u
                tok = jj * (ROW_UNROLL // TOP_K) + u // TOP_K
                _row_copy(ys_hbm, dest_smem[base + j], buf.at[s, u % TOP_K], tok,
                          sem_rows.at[s]).start(priority=u % 2)
            return 0

        lax.fori_loop(0, n // ROW_UNROLL, issue, 0)

    @pl.when(i == 0)
    def _():
        gather_tile(0, 0)

    @pl.when(i < last)
    def _():
        gather_tile(i + 1, 1 - slot)

    for k in range(TOP_K):
        pltpu.make_async_copy(ys_hbm.at[pl.ds(0, tm), :], buf.at[slot, k], sem_rows.at[slot]).wait()
    gates = gate_ref[...]
    y = sum(gates[:, k:k + 1] * buf[slot, k] for k in range(TOP_K))
    o_ref[...] = _layer_norm(alpha * x_ref[...] + y, g_ref[...], b_ref[...])


def _moe(x2, idx, gates, w1, b1, w2, b2, ln_g, ln_b, alpha):
    T, d = x2.shape
    ff = w2.shape[1]
    tk = T * TOP_K
    bm = EXPERT_TILE
    n_tiles = -(-tk // bm) + N_EXPERTS
    R = n_tiles * bm

    tr = min(TOKEN_TILE, T)
    small = lambda tm: pl.BlockSpec((tm, TOP_K), lambda i: (i, 0))
    rank, counts = pl.pallas_call(
        _rank_kernel, grid=(T // tr,),
        in_specs=[small(tr)], out_specs=[small(tr), _const_spec((1, N_EXPERTS))],
        out_shape=[jax.ShapeDtypeStruct((T, TOP_K), I32), jax.ShapeDtypeStruct((1, N_EXPERTS), F32)],
        scratch_shapes=[pltpu.VMEM((1, N_EXPERTS), F32)],
        compiler_params=_params("arbitrary"),
    )(idx)

    counts = counts.reshape(N_EXPERTS).astype(I32)
    padded = (counts + bm - 1) // bm * bm
    pad_ends = jnp.cumsum(padded)
    pad_starts = pad_ends - padded
    tile_row = jnp.arange(n_tiles, dtype=I32) * bm
    tile_e = jnp.minimum(jnp.sum(pad_ends[None, :] <= tile_row[:, None], 1), N_EXPERTS - 1).astype(I32)
    tile_valid = (tile_row < pad_ends[-1]).astype(I32)

    dest = pl.pallas_call(
        _dest_kernel, grid=(T // tr,),
        in_specs=[small(tr), small(tr), _const_spec((1, N_EXPERTS))], out_specs=small(tr),
        out_shape=jax.ShapeDtypeStruct((T, TOP_K), I32),
        compiler_params=_params("arbitrary"),
    )(idx, rank, pad_starts.reshape(1, N_EXPERTS)).reshape(tk)

    tm = min(ROUTE_TILE, T)
    n = tm * TOP_K
    any_spec = pl.BlockSpec(memory_space=pl.ANY)
    last_tile = jnp.where(padded > 0, pad_ends - bm, -1)
    tail = pad_ends[-1] + jnp.arange(n_tiles - tk // bm, dtype=I32) * bm
    tail = jnp.where(tail < R, tail, -1)
    zero_rows = jnp.concatenate([last_tile, tail]).astype(I32)
    xs = pl.pallas_call(
        _dispatch_kernel,
        grid_spec=pltpu.PrefetchScalarGridSpec(
            num_scalar_prefetch=1, grid=(T // tm,),
            in_specs=[any_spec, pl.BlockSpec((tm, d), lambda i, zr: (i, 0))], out_specs=any_spec,
            scratch_shapes=[pltpu.SMEM((2 * n,), I32), pltpu.VMEM((bm, d), F32),
                            pltpu.SemaphoreType.DMA((2,)), pltpu.SemaphoreType.DMA(())]),
        out_shape=jax.ShapeDtypeStruct((R, d), F32),
        compiler_params=_params("arbitrary"),
    )(zero_rows, dest, x2)

    ys = pl.pallas_call(
        _expert_kernel,
        grid_spec=pltpu.PrefetchScalarGridSpec(
            num_scalar_prefetch=2, grid=(n_tiles,),
            in_specs=[pl.BlockSpec((bm, d), lambda i, te, tv: (i, 0)),
                      pl.BlockSpec((None, d, 2 * ff), lambda i, te, tv: (te[i], 0, 0)),
                      pl.BlockSpec((None, 1, 2 * ff), lambda i, te, tv: (te[i], 0, 0)),
                      pl.BlockSpec((None, ff, d), lambda i, te, tv: (te[i], 0, 0)),
                      pl.BlockSpec((None, 1, d), lambda i, te, tv: (te[i], 0, 0))],
            out_specs=pl.BlockSpec((bm, d), lambda i, te, tv: (i, 0))),
        out_shape=jax.ShapeDtypeStruct((R, d), F32),
        compiler_params=_params("arbitrary"),
    )(tile_e, tile_valid, xs, w1.astype(BF16), b1.reshape(N_EXPERTS, 1, 2 * ff),
      w2.astype(BF16), b2.reshape(N_EXPERTS, 1, d))

    return pl.pallas_call(
        functools.partial(_combine_kernel, alpha=alpha), grid=(T // tm,),
        in_specs=[any_spec, any_spec, _token_spec(tm, d), small(tm), _const_spec((1, d)), _const_spec((1, d))],
        out_specs=_token_spec(tm, d),
        out_shape=jax.ShapeDtypeStruct((T, d), F32),
        scratch_shapes=[pltpu.SMEM((2 * n,), I32), pltpu.VMEM((2, TOP_K, tm, d), F32),
                        pltpu.SemaphoreType.DMA((2,)), pltpu.SemaphoreType.DMA((2,))],
        compiler_params=_params("arbitrary"),
    )(dest, ys, x2, gates, ln_g.reshape(1, d), ln_b.reshape(1, d))


def kernel(x, mem, positions, w_in, diff_lambda_q1, diff_lambda_k1, diff_lambda_q2, diff_lambda_k2, diff_subln_g, w_diff_o, gdn_conv_w, gdn_A_log, gdn_dt_bias, gdn_norm_g, w_gdn_o, w_mix_o, ln1_g, ln1_b, w_cq, w_ck, w_cv, w_co, ln2_g, ln2_b, w_router, b_router, w_exp_in, b_exp_in, w_exp_out, b_exp_out, ln3_g, ln3_b):
    B, S, d = x.shape
    depth = w_in.shape[0]
    alpha = (2 * depth) ** 0.25
    rope = _rope_tables(positions)
    xf = x.reshape(B * S, d)
    for l in range(depth):
        lambda_init = 0.8 - 0.6 * math.exp(-0.3 * l)
        dq, dk, dvt, gdn_raw, sz, sga, sgb, beta_t, g_t = _input_projections(
            xf, w_in[l], rope, gdn_A_log[l], gdn_dt_bias[l], B, S)
        lam_vecs = jnp.stack([diff_lambda_q1[l], diff_lambda_k1[l], diff_lambda_q2[l], diff_lambda_k2[l]])
        od = _diff_attention(dq, dk, dvt, lam_vecs, diff_subln_g[l], B, S, lambda_init)
        gq, gk, gv = _gdn_prep(gdn_raw, gdn_conv_w[l], B, S)
        og = _gated_deltanet(gq, gk, gv, sz, beta_t, g_t, gdn_norm_g[l], B, S)
        x1 = _mix(xf, od, og, sga, sgb, w_diff_o[l], w_gdn_o[l], w_mix_o[l], ln1_g[l], ln1_b[l], alpha)
        x2, idx, gates = _cross_attention(x1, mem, w_cq[l], w_ck[l], w_cv[l], w_co[l], ln2_g[l], ln2_b[l],
                                          w_router[l], b_router[l], B, S, alpha)
        xf = _moe(x2, idx, gates, w_exp_in[l], b_exp_in[l], w_exp_out[l], b_exp_out[l],
                  ln3_g[l], ln3_b[l], alpha)
    return xf.reshape(B, S, d)
```

```python
import functools
import math

import jax
import jax.numpy as jnp
from jax import lax
from jax.experimental import pallas as pl
from jax.experimental.pallas import tpu as pltpu

F32 = jnp.float32
BF16 = jnp.bfloat16
I32 = jnp.int32

D_MODEL = 1024
DIFF_HEADS = 8
DIFF_HEAD_DIM = 64
ROPE_THETA = 500000.0
ROT_DIM = DIFF_HEAD_DIM // 4
ROT_HALF = ROT_DIM // 2
GDN_HEADS = 8
GDN_HEAD_DIM = 128
CONV_WIDTH = 4
GDN_CHUNK = 64
MEM_HEADS = 4
MEM_HEAD_DIM = D_MODEL // MEM_HEADS
N_EXPERTS = 32
TOP_K = 4
SWIGLU_LIMIT = 7.0
SWIGLU_ALPHA = 1.702
LANES = 128
SUBLANES = 8
VMEM_LIMIT = 48 * 1024 * 1024

TOKEN_TILE = 512
ATTN_Q_TILE = 256
GDN_GROUP = 4
ROUTE_TILE = 256
EXPERT_TILE = 512
ROW_UNROLL = 32
HIGHEST = lax.Precision.HIGHEST


def _params(*sem):
    return pltpu.CompilerParams(dimension_semantics=sem, vmem_limit_bytes=VMEM_LIMIT)


def _dot(a, b):
    return jnp.dot(a, b, preferred_element_type=F32)


def _dot_nt(a, b):
    return lax.dot_general(a, b, (((1,), (1,)), ((), ())), preferred_element_type=F32)


def _dot_tn(a, b):
    return lax.dot_general(a, b, (((0,), (0,)), ((), ())), preferred_element_type=F32)


def _sigmoid(x):
    return 1.0 / (1.0 + jnp.exp(-x))


def _layer_norm(r, g, b):
    mu = jnp.mean(r, -1, keepdims=True)
    d = r - mu
    var = jnp.mean(d * d, -1, keepdims=True)
    return d * lax.rsqrt(var + 1e-5) * g + b


def _rope_kernel(pos_ref, inv_ref, cos_ref, sa_ref, sb_ref):
    ang = pos_ref[...] * inv_ref[...]
    sin = jnp.sin(ang)
    first_half = lax.broadcasted_iota(I32, ang.shape, 1) % DIFF_HEAD_DIM < ROT_HALF
    cos_ref[...] = jnp.cos(ang)
    sa_ref[...] = jnp.where(first_half, -sin, 0.0)
    sb_ref[...] = jnp.where(first_half, 0.0, sin)


def _rope_tables(positions):
    T = positions.size
    inv_freq = jnp.power(ROPE_THETA, -jnp.arange(0, ROT_DIM, 2, dtype=F32) / ROT_DIM)
    inv = jnp.concatenate([inv_freq, inv_freq, jnp.zeros((DIFF_HEAD_DIM - ROT_DIM,), F32)])
    inv = jnp.tile(inv, LANES // DIFF_HEAD_DIM).reshape(1, LANES)
    pos = jnp.broadcast_to(positions.reshape(T, 1).astype(F32), (T, LANES))
    rb = min(T, 1024)
    spec = pl.BlockSpec((rb, LANES), lambda i: (i, 0))
    return pl.pallas_call(
        _rope_kernel,
        grid=(T // rb,),
        in_specs=[spec, _const_spec((1, LANES))],
        out_specs=[spec] * 3,
        out_shape=[jax.ShapeDtypeStruct((T, LANES), F32)] * 3,
        compiler_params=_params("arbitrary"),
    )(pos, inv)


def _proj_diff_kernel(x_ref, w_ref, wvt_ref, cos_ref, sa_ref, sb_ref, q_ref, k_ref, vt_ref):
    xb = x_ref[...].astype(BF16)
    cos_t, sin_a, sin_b = cos_ref[...], sa_ref[...], sb_ref[...]
    qk_w = DIFF_HEADS * 2 * DIFF_HEAD_DIM
    for part, o_ref in enumerate((q_ref, k_ref)):
        acc = _dot(xb, w_ref[:, part * qk_w:(part + 1) * qk_w])
        for h in range(DIFF_HEADS):
            t = acc[:, h * LANES:(h + 1) * LANES]
            r = (t * cos_t + pltpu.roll(t, LANES - ROT_HALF, 1) * sin_a
                 + pltpu.roll(t, ROT_HALF, 1) * sin_b)
            if part == 0:
                r = r * (DIFF_HEAD_DIM ** -0.5 * math.log2(math.e))
            o_ref[:, h * LANES:(h + 1) * LANES] = r.astype(BF16)
    vt = _dot_nt(wvt_ref[...], xb)
    tk = vt_ref.shape[-1]
    for h in range(DIFF_HEADS):
        for c in range(vt_ref.shape[1]):
            vt_ref[h, c] = vt[h * LANES:(h + 1) * LANES, c * tk:(c + 1) * tk].astype(BF16)


def _proj_plain_kernel(x_ref, w_ref, o_ref):
    o_ref[...] = _dot(x_ref[...].astype(BF16), w_ref[...]).astype(BF16)


def _proj_gdn_kernel(x_ref, w_ref, cw_ref, q_ref, k_ref, v_ref, tail_ref, *, tiles_per_seq):
    xb = x_ref[...].astype(BF16)
    tm = xb.shape[0]
    d = q_ref.shape[1]
    first = pl.program_id(0) % tiles_per_seq == 0
    row8 = lax.broadcasted_iota(I32, (SUBLANES, d), 0)
    for part, o_ref in enumerate((q_ref, k_ref, v_ref)):
        cols = slice(part * d, (part + 1) * d)
        acc = _dot(xb, w_ref[:, cols])
        prev = jnp.where(first, 0.0, tail_ref[:, cols])
        tail_ref[:, cols] = acc[tm - SUBLANES:, :]
        y = acc * cw_ref[CONV_WIDTH - 1:CONV_WIDTH, cols]
        head = y[:SUBLANES]
        for j in range(1, CONV_WIDTH):
            cw_j = cw_ref[CONV_WIDTH - 1 - j:CONV_WIDTH - j, cols]
            shifted = pltpu.roll(acc, j, 0)
            y = y + shifted * cw_j
            head = head + jnp.where(row8 >= j, shifted[:SUBLANES], pltpu.roll(prev, j, 0)) * cw_j
        y = jnp.concatenate([head, y[SUBLANES:]], 0)
        y = y * _sigmoid(y)
        if part == 2:
            o_ref[...] = y.astype(BF16)
            continue
        for h in range(d // GDN_HEAD_DIM):
            lanes = slice(h * GDN_HEAD_DIM, (h + 1) * GDN_HEAD_DIM)
            yh = y[:, lanes]
            yh = yh * lax.rsqrt(jnp.sum(yh * yh, -1, keepdims=True) + 1e-6)
            if part == 0:
                yh = yh * (GDN_HEAD_DIM ** -0.5)
            o_ref[:, lanes] = yh.astype(BF16)


def _proj_gate_kernel(x_ref, w_ref, sz_ref, ga_ref, gb_ref):
    xb = x_ref[...].astype(BF16)
    z = _dot(xb, w_ref[:, :D_MODEL])
    sz_ref[...] = (z * _sigmoid(z)).astype(BF16)
    ga_ref[...] = _sigmoid(_dot(xb, w_ref[:, D_MODEL:2 * D_MODEL])).astype(BF16)
    gb_ref[...] = _sigmoid(_dot(xb, w_ref[:, 2 * D_MODEL:])).astype(BF16)


def _decay_kernel(x_ref, wt_ref, alog_ref, dtb_ref, beta_ref, g_ref):
    ba = lax.dot_general(wt_ref[...], x_ref[...], (((1,), (1,)), ((), ())),
                         precision=HIGHEST, preferred_element_type=F32)
    b, a = ba[:GDN_HEADS], ba[GDN_HEADS:]
    beta_ref[...] = _sigmoid(b)
    s = a + dtb_ref[...]
    softplus = jnp.maximum(s, 0.0) + jnp.log1p(jnp.exp(-jnp.abs(s)))
    g = -jnp.exp(alog_ref[...]) * softplus
    tm = g.shape[1]
    r = lax.broadcasted_iota(I32, (tm, tm), 0)
    c = lax.broadcasted_iota(I32, (tm, tm), 1)
    upper = ((r <= c) & (r // GDN_CHUNK == c // GDN_CHUNK)).astype(F32)
    g_ref[...] = jnp.dot(g, upper, precision=HIGHEST, preferred_element_type=F32)


def _token_spec(tm, width):
    return pl.BlockSpec((tm, width), lambda i: (i, 0))


def _const_spec(shape):
    return pl.BlockSpec(shape, lambda *_: (0,) * len(shape))


def _input_projections(xf, w_in, rope, conv_w, a_log, dt_bias, B, S):
    T = xf.shape[0]
    tm = min(TOKEN_TILE, S)
    grid = (T // tm,)
    d = D_MODEL
    w_diff = w_in[:, :2 * d].astype(BF16)
    w_vt = w_in[:, 2 * d:3 * d].T.astype(BF16)
    w_gdn = w_in[:, 3 * d:6 * d].astype(BF16)
    w_z = w_in[:, 6 * d:7 * d]
    w_ba_t = w_in[:, 7 * d:7 * d + 2 * GDN_HEADS].T
    w_gate = jnp.concatenate([w_z, w_in[:, 7 * d + 2 * GDN_HEADS:]], 1).astype(BF16)
    x_spec = _token_spec(tm, d)
    act = jax.ShapeDtypeStruct((T, d), BF16)

    tk = min(ATTN_Q_TILE, S)
    nt = S // tm
    dq, dk, dvt = pl.pallas_call(
        _proj_diff_kernel, grid=grid,
        in_specs=[x_spec, _const_spec((d, 2 * d)), _const_spec((d, d))] + [_token_spec(tm, LANES)] * 3,
        out_specs=[_token_spec(tm, d), _token_spec(tm, d),
                   pl.BlockSpec((None, DIFF_HEADS, tm // tk, LANES, tk),
                                lambda i: (i // nt, 0, i % nt, 0, 0))],
        out_shape=[act, act, jax.ShapeDtypeStruct((B, DIFF_HEADS, S // tk, LANES, tk), BF16)],
        compiler_params=_params("arbitrary"),
    )(xf, w_diff, w_vt, *rope)

    gq, gk, gv = pl.pallas_call(
        functools.partial(_proj_gdn_kernel, tiles_per_seq=nt), grid=grid,
        in_specs=[x_spec, _const_spec((d, 3 * d)), _const_spec((CONV_WIDTH, 3 * d))],
        out_specs=[_token_spec(tm, d)] * 3, out_shape=[act] * 3,
        scratch_shapes=[pltpu.VMEM((SUBLANES, 3 * d), F32)],
        compiler_params=_params("arbitrary"),
    )(xf, w_gdn, conv_w)

    sz, sga, sgb = pl.pallas_call(
        _proj_gate_kernel, grid=grid,
        in_specs=[x_spec, _const_spec((d, 3 * d))],
        out_specs=[_token_spec(tm, d)] * 3, out_shape=[act] * 3,
        compiler_params=_params("arbitrary"),
    )(xf, w_gate)

    td = min(256, T)
    lane_spec = pl.BlockSpec((GDN_HEADS, td), lambda i: (0, i))
    beta_t, g_t = pl.pallas_call(
        _decay_kernel, grid=(T // td,),
        in_specs=[_token_spec(td, d), _const_spec((2 * GDN_HEADS, d)),
                  _const_spec((GDN_HEADS, 1)), _const_spec((GDN_HEADS, 1))],
        out_specs=[lane_spec] * 2,
        out_shape=[jax.ShapeDtypeStruct((GDN_HEADS, T), F32)] * 2,
        compiler_params=_params("arbitrary"),
    )(xf, w_ba_t, a_log.reshape(GDN_HEADS, 1), dt_bias.reshape(GDN_HEADS, 1))
    gdn_qkv = tuple(t.reshape(B, S, d) for t in (gq, gk, gv))
    return dq, dk, dvt, gdn_qkv, sz, sga, sgb, beta_t, g_t


def _diff_attn_kernel(lam_ref, g_ref, q_ref, k_ref, vt_ref, o_ref,
                      sa_ref, sb_ref, m_ref, l_ref, acc_ref, *, lambda_init, tq):
    lv = lam_ref[...]
    lam = (jnp.exp(jnp.sum(lv[0:1] * lv[1:2], -1, keepdims=True))
           - jnp.exp(jnp.sum(lv[2:3] * lv[3:4], -1, keepdims=True)) + lambda_init)
    lane = lax.broadcasted_iota(I32, (tq, LANES), 1)
    key = lax.broadcasted_iota(I32, (tq, 2 * tq), 0)
    qry = lax.broadcasted_iota(I32, (tq, 2 * tq), 1)
    causal = key <= jnp.where(qry >= tq, qry - tq, qry)

    for qi in range(q_ref.shape[0] // tq):
        q = q_ref[qi * tq:(qi + 1) * tq, :]
        zero = jnp.zeros_like(q)
        q2 = jnp.concatenate([jnp.where(lane < DIFF_HEAD_DIM, q, zero),
                              jnp.where(lane >= DIFF_HEAD_DIM, q, zero)], 0)

        def scores(j, q2=q2):
            return _dot_nt(k_ref[pl.ds(pl.multiple_of(j * tq, tq), tq), :], q2)

        def update(s, j):
            m = m_ref[...]
            m_new = jnp.maximum(m, jnp.max(s, 0, keepdims=True))
            alpha = jnp.exp2(m - m_new)
            p = jnp.exp2(s - m_new)
            l_ref[...] = alpha * l_ref[...] + jnp.sum(p, 0, keepdims=True)
            acc_ref[...] = alpha * acc_ref[...] + _dot(vt_ref[j], p.astype(BF16))
            m_ref[...] = m_new

        m_ref[...] = jnp.full(m_ref.shape, -jnp.inf, F32)
        l_ref[...] = jnp.zeros_like(l_ref)
        acc_ref[...] = jnp.zeros_like(acc_ref)
        sa_ref[...] = scores(0)

        def pair(i, _, scores=scores, update=update):
            j = 2 * i
            sb_ref[...] = scores(j + 1)
            update(sa_ref[...], j)
            sa_ref[...] = scores(j + 2)
            update(sb_ref[...], j + 1)
            return 0

        if qi >= 2:
            lax.fori_loop(0, qi // 2, pair, 0)
        if qi % 2 == 1:
            sb_ref[...] = scores(qi)
            update(sa_ref[...], qi - 1)
            update(jnp.where(causal, sb_ref[...], -jnp.inf), qi)
        else:
            update(jnp.where(causal, sa_ref[...], -jnp.inf), qi)

        o = acc_ref[...] / l_ref[...]
        o = o[:, :tq] - lam * o[:, tq:]
        o = o * lax.rsqrt(jnp.mean(o * o, 0, keepdims=True) + 1e-6) * g_ref[...]
        o_ref[qi * tq:(qi + 1) * tq, :] = (o * (1.0 - lambda_init)).T.astype(BF16)


def _diff_attention(dq, dk, dvt, lam_vecs, subln_g, B, S, lambda_init):
    tq = min(ATTN_Q_TILE, S)
    d = D_MODEL
    q3, k3 = dq.reshape(B, S, d), dk.reshape(B, S, d)
    seq = pl.BlockSpec((None, S, LANES), lambda b, h: (b, 0, h))
    out = pl.pallas_call(
        functools.partial(_diff_attn_kernel, lambda_init=lambda_init, tq=tq),
        grid=(B, DIFF_HEADS),
        in_specs=[_const_spec((4, DIFF_HEAD_DIM)), _const_spec((LANES, 1)), seq, seq,
                  pl.BlockSpec((None, None, S // tq, LANES, tq), lambda b, h: (b, h, 0, 0, 0))],
        out_specs=seq,
        out_shape=jax.ShapeDtypeStruct((B, S, d), BF16),
        scratch_shapes=[pltpu.VMEM((tq, 2 * tq), F32), pltpu.VMEM((tq, 2 * tq), F32),
                        pltpu.VMEM((1, 2 * tq), F32), pltpu.VMEM((1, 2 * tq), F32),
                        pltpu.VMEM((LANES, 2 * tq), F32)],
        compiler_params=_params("arbitrary", "arbitrary"),
    )(lam_vecs, subln_g.reshape(LANES, 1), q3, k3, dvt)
    return out.reshape(B * S, d)


def _gdn_local_kernel(q_ref, k_ref, v_ref, col_ref, row_ref,
                      u_ref, w_ref, qd_ref, kd_ref, qk_ref, gl_ref):
    C = GDN_CHUNK
    R = q_ref.shape[0]
    ri = lax.broadcasted_iota(I32, (R, R), 0)
    ci = lax.broadcasted_iota(I32, (R, R), 1)
    same = (ri // C) == (ci // C)
    incl = same & (ri >= ci)
    strict = same & (ri > ci)
    eye = (ri == ci).astype(F32)
    heads = range(GDN_HEADS)
    head_lanes = [slice(h * GDN_HEAD_DIM, (h + 1) * GDN_HEAD_DIM) for h in heads]
    decays, pbs, t_mats = [], [], []
    for h in heads:
        kh = k_ref[:, head_lanes[h]]
        g_c = col_ref[:, GDN_HEADS + h:GDN_HEADS + h + 1]
        g_r = row_ref[GDN_HEADS + h:GDN_HEADS + h + 1, :]
        decay = jnp.where(incl, jnp.exp(jnp.where(incl, g_c - g_r, 0.0)), 0.0)
        p = -(jnp.where(strict, _dot_nt(kh, kh) * decay, 0.0) * col_ref[:, h:h + 1])
        decays.append(decay)
        pbs.append(p.astype(BF16))
        t_mats.append(eye + p)
    for _ in range(5):
        pbs = [_dot(pb, pb).astype(BF16) for pb in pbs]
        t_mats = [t + _dot(t.astype(BF16), pb) for t, pb in zip(t_mats, pbs)]
    for h in heads:
        lanes = head_lanes[h]
        qh, kh, vh = q_ref[:, lanes], k_ref[:, lanes], v_ref[:, lanes]
        beta_c = col_ref[:, h:h + 1]
        g_c = col_ref[:, GDN_HEADS + h:GDN_HEADS + h + 1]
        decay = decays[h]
        eg_c = jnp.exp(g_c)
        kf = kh.astype(F32)
        rhs = jnp.concatenate([vh.astype(F32) * beta_c, kf * (beta_c * eg_c)], 1)
        uw = _dot(t_mats[h].astype(BF16), rhs.astype(BF16))
        u_ref[:, lanes] = uw[:, :GDN_HEAD_DIM].astype(BF16)
        w_ref[:, lanes] = uw[:, GDN_HEAD_DIM:].astype(BF16)
        qd_ref[:, lanes] = (qh.astype(F32) * eg_c).astype(BF16)
        for c in range(R // C):
            rows = slice(c * C, (c + 1) * C)
            g_last = g_c[(c + 1) * C - 1:(c + 1) * C, :]
            kd_ref[rows, lanes] = (kf[rows] * jnp.exp(g_last - g_c[rows])).astype(BF16)
            gl_ref[c, h:h + 1, :] = jnp.broadcast_to(jnp.exp(g_last), (1, GDN_HEAD_DIM))
            qk_ref[c, h] = (_dot_nt(qh[rows], kh[rows]) * decay[rows, rows]).astype(BF16)


def _gdn_scan_kernel(u_ref, w_ref, qd_ref, kd_ref, qk_ref, gl_ref, sz_ref, ng_ref, o_ref, state_ref):
    @pl.when(pl.program_id(1) == 0)
    def _():
        state_ref[...] = jnp.zeros_like(state_ref)

    C = GDN_CHUNK
    ng = ng_ref[...]
    heads = range(GDN_HEADS)
    lanes = [slice(h * GDN_HEAD_DIM, (h + 1) * GDN_HEAD_DIM) for h in heads]
    for c in range(u_ref.shape[0] // C):
        rows = slice(c * C, (c + 1) * C)
        st = [state_ref[h] for h in heads]
        ws = [_dot(jnp.concatenate([w_ref[rows, lanes[h]], qd_ref[rows, lanes[h]]], 0),
                   st[h].astype(BF16)) for h in heads]
        vnb = [(u_ref[rows, lanes[h]].astype(F32) - ws[h][:C]).astype(BF16) for h in heads]
        for h in heads:
            state_ref[h] = st[h] * gl_ref[c, h:h + 1, :] + _dot_tn(kd_ref[rows, lanes[h]], vnb[h])
        for h in heads:
            o = ws[h][C:] + _dot(qk_ref[c, h], vnb[h])
            o = o * lax.rsqrt(jnp.mean(o * o, -1, keepdims=True) + 1e-6) * ng
            o_ref[rows, lanes[h]] = (o * sz_ref[rows, lanes[h]].astype(F32)).astype(BF16)


def _gated_deltanet(gq, gk, gv, sz, beta_t, g_t, norm_g, B, S):
    d = D_MODEL
    C = GDN_CHUNK
    R = min(GDN_GROUP * C, S)
    nc = R // C
    row_arr = jnp.concatenate([beta_t, g_t], 0).reshape(2 * GDN_HEADS, B, S).transpose(1, 0, 2)
    col_arr = row_arr.transpose(0, 2, 1)
    blk = pl.BlockSpec((None, R, d), lambda b, i: (b, i, 0))
    qk_blk = pl.BlockSpec((None, nc, GDN_HEADS, C, C), lambda b, i: (b, i, 0, 0, 0))
    gl_blk = pl.BlockSpec((None, nc, GDN_HEADS, GDN_HEAD_DIM), lambda b, i: (b, i, 0, 0))
    act = jax.ShapeDtypeStruct((B, S, d), BF16)
    u, w, qd, kd, qk, gl = pl.pallas_call(
        _gdn_local_kernel,
        grid=(B, S // R),
        in_specs=[blk, blk, blk,
                  pl.BlockSpec((None, R, 2 * GDN_HEADS), lambda b, i: (b, i, 0)),
                  pl.BlockSpec((None, 2 * GDN_HEADS, R), lambda b, i: (b, 0, i))],
        out_specs=[blk, blk, blk, blk, qk_blk, gl_blk],
        out_shape=[act, act, act, act,
                   jax.ShapeDtypeStruct((B, S // C, GDN_HEADS, C, C), BF16),
                   jax.ShapeDtypeStruct((B, S // C, GDN_HEADS, GDN_HEAD_DIM), F32)],
        compiler_params=_params("arbitrary", "arbitrary"),
    )(gq, gk, gv, col_arr, row_arr)
    out = pl.pallas_call(
        _gdn_scan_kernel,
        grid=(B, S // R),
        in_specs=[blk, blk, blk, blk, qk_blk, gl_blk, blk, _const_spec((1, GDN_HEAD_DIM))],
        out_specs=blk,
        out_shape=act,
        scratch_shapes=[pltpu.VMEM((GDN_HEADS, GDN_HEAD_DIM, GDN_HEAD_DIM), F32)],
        compiler_params=_params("arbitrary", "arbitrary"),
    )(u, w, qd, kd, qk, gl, sz.reshape(B, S, d), norm_g.reshape(1, GDN_HEAD_DIM))
    return out.reshape(B * S, d)


def _mix_kernel(x_ref, od_ref, og_ref, ga_ref, gb_ref, wd_ref, wg_ref, wm_ref, g_ref, b_ref,
                o_ref, *, alpha):
    yd = _dot(od_ref[...], wd_ref[...])
    yg = _dot(og_ref[...], wg_ref[...])
    m = ga_ref[...].astype(F32) * yd + gb_ref[...].astype(F32) * yg
    r = alpha * x_ref[...] + _dot(m.astype(BF16), wm_ref[...])
    o_ref[...] = _layer_norm(r, g_ref[...], b_ref[...])


def _mix(xf, od, og, sga, sgb, w_diff_o, w_gdn_o, w_mix_o, ln_g, ln_b, alpha):
    T, d = xf.shape
    tm = min(TOKEN_TILE, T)
    tok = _token_spec(tm, d)
    wspec = _const_spec((d, d))
    vec = _const_spec((1, d))
    return pl.pallas_call(
        functools.partial(_mix_kernel, alpha=alpha), grid=(T // tm,),
        in_specs=[tok] * 5 + [wspec] * 3 + [vec] * 2,
        out_specs=tok, out_shape=jax.ShapeDtypeStruct((T, d), F32),
        compiler_params=_params("arbitrary"),
    )(xf, od, og, sga, sgb, w_diff_o.astype(BF16), w_gdn_o.astype(BF16), w_mix_o.astype(BF16),
      ln_g.reshape(1, d), ln_b.reshape(1, d))


def _cross_kernel(x_ref, k_ref, v_ref, wq_ref, wo_ref, g_ref, b_ref, wr_ref, br_ref,
                  o_ref, idx_ref, gate_ref, *, alpha):
    half = x_ref.shape[0] // 2
    for part in range(2):
        rows = slice(part * half, (part + 1) * half)
        x = x_ref[rows, :]
        q = (_dot(x.astype(BF16), wq_ref[...]) * (MEM_HEAD_DIM ** -0.5)).astype(BF16)
        outs = []
        for h in range(MEM_HEADS):
            lanes = slice(h * MEM_HEAD_DIM, (h + 1) * MEM_HEAD_DIM)
            s = _dot_nt(q[:, lanes], k_ref[:, lanes])
            p = jnp.exp(s - jnp.max(s, -1, keepdims=True))
            p = p / jnp.sum(p, -1, keepdims=True)
            outs.append(_dot(p.astype(BF16), v_ref[:, lanes]))
        o = jnp.concatenate(outs, 1).astype(BF16)
        r = alpha * x + _dot(o, wo_ref[...])
        x2 = _layer_norm(r, g_ref[...], b_ref[...])
        o_ref[rows, :] = x2

        logits = _dot(x2.astype(BF16), wr_ref[...]) + br_ref[...]
        lane = lax.broadcasted_iota(I32, logits.shape, 1)
        vals, idxs = [], []
        for _ in range(TOP_K):
            mx = jnp.max(logits, -1, keepdims=True)
            ix = jnp.min(jnp.where(logits == mx, lane, N_EXPERTS), -1, keepdims=True)
            vals.append(mx)
            idxs.append(ix)
            logits = jnp.where(lane == ix, -jnp.inf, logits)
        e = jnp.exp(jnp.concatenate(vals, 1) - vals[0])
        gate_ref[rows, :] = e / jnp.sum(e, -1, keepdims=True)
        idx_ref[rows, :] = jnp.concatenate(idxs, 1)


def _cross_attention(x1, mem, w_cq, w_ck, w_cv, w_co, ln_g, ln_b, w_router, b_router, B, S, alpha):
    T, d = x1.shape
    M = mem.shape[1]
    w_kv = jnp.concatenate([w_ck, w_cv], 1).astype(BF16)
    tmem = min(TOKEN_TILE, B * M)
    kv = pl.pallas_call(
        _proj_plain_kernel, grid=(B * M // tmem,),
        in_specs=[_token_spec(tmem, d), _const_spec((d, 2 * d))],
        out_specs=_token_spec(tmem, 2 * d),
        out_shape=jax.ShapeDtypeStruct((B * M, 2 * d), BF16),
        compiler_params=_params("arbitrary"),
    )(mem.reshape(B * M, d), w_kv).reshape(B, M, 2 * d)

    tm = min(TOKEN_TILE, S)
    nt = S // tm
    tok = pl.BlockSpec((tm, d), lambda b, i: (b * nt + i, 0))
    small = pl.BlockSpec((tm, TOP_K), lambda b, i: (b * nt + i, 0))
    return pl.pallas_call(
        functools.partial(_cross_kernel, alpha=alpha), grid=(B, nt),
        in_specs=[tok,
                  pl.BlockSpec((None, M, d), lambda b, i: (b, 0, 0)),
                  pl.BlockSpec((None, M, d), lambda b, i: (b, 0, 1)),
                  _const_spec((d, d)), _const_spec((d, d)), _const_spec((1, d)), _const_spec((1, d)),
                  _const_spec((d, N_EXPERTS)), _const_spec((1, N_EXPERTS))],
        out_specs=[tok, small, small],
        out_shape=[jax.ShapeDtypeStruct((T, d), F32), jax.ShapeDtypeStruct((T, TOP_K), I32),
                   jax.ShapeDtypeStruct((T, TOP_K), F32)],
        compiler_params=_params("arbitrary", "arbitrary"),
    )(x1, kv, kv, w_cq.astype(BF16), w_co.astype(BF16), ln_g.reshape(1, d), ln_b.reshape(1, d),
      w_router.astype(BF16), b_router.reshape(1, N_EXPERTS))


def _rank_kernel(idx_ref, rank_ref, cnt_ref, run_ref):
    @pl.when(pl.program_id(0) == 0)
    def _():
        run_ref[...] = jnp.zeros_like(run_ref)

    idx = idx_ref[...]
    tm = idx.shape[0]
    lane = lax.broadcasted_iota(I32, (tm, N_EXPERTS), 1)
    hits = [lane == idx[:, k:k + 1] for k in range(TOP_K)]
    onehot = sum(hit.astype(F32) for hit in hits)
    r = lax.broadcasted_iota(I32, (tm, tm), 0)
    c = lax.broadcasted_iota(I32, (tm, tm), 1)
    lower = (r > c).astype(BF16)
    before = _dot(lower, onehot.astype(BF16)) + run_ref[...]
    ranks = [jnp.sum(jnp.where(hit, before, 0.0), -1, keepdims=True) for hit in hits]
    rank_ref[...] = jnp.concatenate(ranks, 1).astype(I32)
    run_ref[...] = run_ref[...] + jnp.sum(onehot, 0, keepdims=True)
    cnt_ref[...] = run_ref[...]


def _dest_kernel(idx_ref, rank_ref, start_ref, dest_ref):
    idx = idx_ref[...]
    lane = lax.broadcasted_iota(I32, (idx.shape[0], N_EXPERTS), 1)
    starts = start_ref[...]
    cols = [jnp.sum(jnp.where(lane == idx[:, k:k + 1], starts, 0), -1, keepdims=True)
            for k in range(TOP_K)]
    dest_ref[...] = rank_ref[...] + jnp.concatenate(cols, 1)


def _row_copy(src_ref, src_row, dst_ref, dst_row, sem):
    return pltpu.make_async_copy(src_ref.at[pl.ds(src_row, 1), :], dst_ref.at[pl.ds(dst_row, 1), :], sem)


def _index_copy(dest_hbm, dest_smem, sem_idx, step, slot, n):
    return pltpu.make_async_copy(dest_hbm.at[pl.ds(pl.multiple_of(step * n, n), n)],
                                 dest_smem.at[pl.ds(pl.multiple_of(slot * n, n), n)], sem_idx.at[slot])


def _dispatch_kernel(zero_rows, dest_hbm, x_ref, xs_hbm, dest_smem, zero_buf, sem_idx, sem_rows):
    tm = x_ref.shape[0]
    n = tm * TOP_K
    i = pl.program_id(0)
    last = pl.num_programs(0) - 1
    slot = i % 2

    @pl.when(i == 0)
    def _():
        _index_copy(dest_hbm, dest_smem, sem_idx, 0, 0, n).start()
        zero_buf[...] = jnp.zeros_like(zero_buf)
        bm = zero_buf.shape[0]

        def zero_tile(t):
            row = pl.multiple_of(jnp.maximum(zero_rows[t], 0), bm)
            return pltpu.make_async_copy(zero_buf, xs_hbm.at[pl.ds(row, bm), :], sem_rows)

        for t in range(zero_rows.shape[0]):
            pl.when(zero_rows[t] >= 0)(lambda t=t: zero_tile(t).start())
        for t in range(zero_rows.shape[0]):
            pl.when(zero_rows[t] >= 0)(lambda t=t: zero_tile(t).wait())

    _index_copy(dest_hbm, dest_smem, sem_idx, i, slot, n).wait()

    @pl.when(i < last)
    def _():
        _index_copy(dest_hbm, dest_smem, sem_idx, i + 1, 1 - slot, n).start()

    base = slot * n

    def issue(jj, _):
        for u in range(ROW_UNROLL):
            j = jj * ROW_UNROLL + u
            tok = jj * (ROW_UNROLL // TOP_K) + u // TOP_K
            _row_copy(x_ref, tok, xs_hbm, dest_smem[base + j], sem_rows).start(priority=u % 2)
        return 0

    lax.fori_loop(0, n // ROW_UNROLL, issue, 0)
    for _ in range(TOP_K):
        pltpu.make_async_copy(x_ref, xs_hbm.at[pl.ds(0, tm), :], sem_rows).wait()


def _expert_kernel(te_ref, tv_ref, xs_ref, w1_ref, b1_ref, w2_ref, b2_ref, ys_ref):
    i = pl.program_id(0)

    @pl.when(tv_ref[i] != 0)
    def _():
        ff = w2_ref.shape[0]
        h = _dot(xs_ref[...].astype(BF16), w1_ref[...]) + b1_ref[...]
        h_gate = jnp.minimum(h[:, :ff], SWIGLU_LIMIT)
        h_up = jnp.clip(h[:, ff:], -SWIGLU_LIMIT, SWIGLU_LIMIT)
        act = h_gate * _sigmoid(SWIGLU_ALPHA * h_gate) * (h_up + 1.0)
        ys_ref[...] = _dot(act.astype(BF16), w2_ref[...]) + b2_ref[...]

    @pl.when(tv_ref[i] == 0)
    def _():
        ys_ref[...] = jnp.zeros_like(ys_ref)


def _combine_kernel(dest_hbm, ys_hbm, x_ref, gate_ref, g_ref, b_ref, o_ref,
                    dest_smem, buf, sem_idx, sem_rows, *, alpha):
    tm = x_ref.shape[0]
    n = tm * TOP_K
    i = pl.program_id(0)
    last = pl.num_programs(0) - 1
    slot = i % 2

    def gather_tile(step, s):
        cp = _index_copy(dest_hbm, dest_smem, sem_idx, step, s, n)
        cp.start()
        cp.wait()
        base = s * n

        def issue(jj, _):
            for u in range(ROW_UNROLL):
                j = jj * ROW_UNROLL + u
                tok = jj * (ROW_UNROLL // TOP_K) + u // TOP_K
                _row_copy(ys_hbm, dest_smem[base + j], buf.at[s, u % TOP_K], tok,
                          sem_rows.at[s]).start(priority=u % 2)
            return 0

        lax.fori_loop(0, n // ROW_UNROLL, issue, 0)

    @pl.when(i == 0)
    def _():
        gather_tile(0, 0)

    @pl.when(i < last)
    def _():
        gather_tile(i + 1, 1 - slot)

    for k in range(TOP_K):
        pltpu.make_async_copy(ys_hbm.at[pl.ds(0, tm), :], buf.at[slot, k], sem_rows.at[slot]).wait()
    gates = gate_ref[...]
    y = sum(gates[:, k:k + 1] * buf[slot, k] for k in range(TOP_K))
    o_ref[...] = _layer_norm(alpha * x_ref[...] + y, g_ref[...], b_ref[...])


def _moe(x2, idx, gates, w1, b1, w2, b2, ln_g, ln_b, alpha):
    T, d = x2.shape
    ff = w2.shape[1]
    tk = T * TOP_K
    bm = EXPERT_TILE
    n_tiles = -(-tk // bm) + N_EXPERTS
    R = n_tiles * bm

    tr = min(TOKEN_TILE, T)
    small = lambda tm: pl.BlockSpec((tm, TOP_K), lambda i: (i, 0))
    rank, counts = pl.pallas_call(
        _rank_kernel, grid=(T // tr,),
        in_specs=[small(tr)], out_specs=[small(tr), _const_spec((1, N_EXPERTS))],
        out_shape=[jax.ShapeDtypeStruct((T, TOP_K), I32), jax.ShapeDtypeStruct((1, N_EXPERTS), F32)],
        scratch_shapes=[pltpu.VMEM((1, N_EXPERTS), F32)],
        compiler_params=_params("arbitrary"),
    )(idx)

    counts = counts.reshape(N_EXPERTS).astype(I32)
    padded = (counts + bm - 1) // bm * bm
    pad_ends = jnp.cumsum(padded)
    pad_starts = pad_ends - padded
    tile_row = jnp.arange(n_tiles, dtype=I32) * bm
    tile_e = jnp.minimum(jnp.sum(pad_ends[None, :] <= tile_row[:, None], 1), N_EXPERTS - 1).astype(I32)
    tile_valid = (tile_row < pad_ends[-1]).astype(I32)

    dest = pl.pallas_call(
        _dest_kernel, grid=(T // tr,),
        in_specs=[small(tr), small(tr), _const_spec((1, N_EXPERTS))], out_specs=small(tr),
        out_shape=jax.ShapeDtypeStruct((T, TOP_K), I32),
        compiler_params=_params("arbitrary"),
    )(idx, rank, pad_starts.reshape(1, N_EXPERTS)).reshape(tk)

    tm = min(ROUTE_TILE, T)
    n = tm * TOP_K
    any_spec = pl.BlockSpec(memory_space=pl.ANY)
    last_tile = jnp.where(padded > 0, pad_ends - bm, -1)
    tail = pad_ends[-1] + jnp.arange(n_tiles - tk // bm, dtype=I32) * bm
    tail = jnp.where(tail < R, tail, -1)
    zero_rows = jnp.concatenate([last_tile, tail]).astype(I32)
    xs = pl.pallas_call(
        _dispatch_kernel,
        grid_spec=pltpu.PrefetchScalarGridSpec(
            num_scalar_prefetch=1, grid=(T // tm,),
            in_specs=[any_spec, pl.BlockSpec((tm, d), lambda i, zr: (i, 0))], out_specs=any_spec,
            scratch_shapes=[pltpu.SMEM((2 * n,), I32), pltpu.VMEM((bm, d), F32),
                            pltpu.SemaphoreType.DMA((2,)), pltpu.SemaphoreType.DMA(())]),
        out_shape=jax.ShapeDtypeStruct((R, d), F32),
        compiler_params=_params("arbitrary"),
    )(zero_rows, dest, x2)

    ys = pl.pallas_call(
        _expert_kernel,
        grid_spec=pltpu.PrefetchScalarGridSpec(
            num_scalar_prefetch=2, grid=(n_tiles,),
            in_specs=[pl.BlockSpec((bm, d), lambda i, te, tv: (i, 0)),
                      pl.BlockSpec((None, d, 2 * ff), lambda i, te, tv: (te[i], 0, 0)),
                      pl.BlockSpec((None, 1, 2 * ff), lambda i, te, tv: (te[i], 0, 0)),
                      pl.BlockSpec((None, ff, d), lambda i, te, tv: (te[i], 0, 0)),
                      pl.BlockSpec((None, 1, d), lambda i, te, tv: (te[i], 0, 0))],
            out_specs=pl.BlockSpec((bm, d), lambda i, te, tv: (i, 0))),
        out_shape=jax.ShapeDtypeStruct((R, d), F32),
        compiler_params=_params("arbitrary"),
    )(tile_e, tile_valid, xs, w1.astype(BF16), b1.reshape(N_EXPERTS, 1, 2 * ff),
      w2.astype(BF16), b2.reshape(N_EXPERTS, 1, d))

    return pl.pallas_call(
        functools.partial(_combine_kernel, alpha=alpha), grid=(T // tm,),
        in_specs=[any_spec, any_spec, _token_spec(tm, d), small(tm), _const_spec((1, d)), _const_spec((1, d))],
        out_specs=_token_spec(tm, d),
        out_shape=jax.ShapeDtypeStruct((T, d), F32),
        scratch_shapes=[pltpu.SMEM((2 * n,), I32), pltpu.VMEM((2, TOP_K, tm, d), F32),
                        pltpu.SemaphoreType.DMA((2,)), pltpu.SemaphoreType.DMA((2,))],
        compiler_params=_params("arbitrary"),
    )(dest, ys, x2, gates, ln_g.reshape(1, d), ln_b.reshape(1, d))


def kernel(x, mem, positions, w_in, diff_lambda_q1, diff_lambda_k1, diff_lambda_q2, diff_lambda_k2, diff_subln_g, w_diff_o, gdn_conv_w, gdn_A_log, gdn_dt_bias, gdn_norm_g, w_gdn_o, w_mix_o, ln1_g, ln1_b, w_cq, w_ck, w_cv, w_co, ln2_g, ln2_b, w_router, b_router, w_exp_in, b_exp_in, w_exp_out, b_exp_out, ln3_g, ln3_b):
    B, S, d = x.shape
    depth = w_in.shape[0]
    alpha = (2 * depth) ** 0.25
    rope = _rope_tables(positions)
    xf = x.reshape(B * S, d)
    for l in range(depth):
        lambda_init = 0.8 - 0.6 * math.exp(-0.3 * l)
        dq, dk, dvt, (gq, gk, gv), sz, sga, sgb, beta_t, g_t = _input_projections(
            xf, w_in[l], rope, gdn_conv_w[l], gdn_A_log[l], gdn_dt_bias[l], B, S)
        lam_vecs = jnp.stack([diff_lambda_q1[l], diff_lambda_k1[l], diff_lambda_q2[l], diff_lambda_k2[l]])
        od = _diff_attention(dq, dk, dvt, lam_vecs, diff_subln_g[l], B, S, lambda_init)
        og = _gated_deltanet(gq, gk, gv, sz, beta_t, g_t, gdn_norm_g[l], B, S)
        x1 = _mix(xf, od, og, sga, sgb, w_diff_o[l], w_gdn_o[l], w_mix_o[l], ln1_g[l], ln1_b[l], alpha)
        x2, idx, gates = _cross_attention(x1, mem, w_cq[l], w_ck[l], w_cv[l], w_co[l], ln2_g[l], ln2_b[l],
                                          w_router[l], b_router[l], B, S, alpha)
        xf = _moe(x2, idx, gates, w_exp_in[l], b_exp_in[l], w_exp_out[l], b_exp_out[l],
                  ln3_g[l], ln3_b[l], alpha)
    return xf.reshape(B, S, d)
```

```python
import functools
import math

import jax
import jax.numpy as jnp
from jax import lax
from jax.experimental import pallas as pl
from jax.experimental.pallas import tpu as pltpu

F32 = jnp.float32
BF16 = jnp.bfloat16
I32 = jnp.int32

D_MODEL = 1024
DIFF_HEADS = 8
DIFF_HEAD_DIM = 64
ROPE_THETA = 500000.0
ROT_DIM = DIFF_HEAD_DIM // 4
ROT_HALF = ROT_DIM // 2
GDN_HEADS = 8
GDN_HEAD_DIM = 128
CONV_WIDTH = 4
GDN_CHUNK = 64
MEM_HEADS = 4
MEM_HEAD_DIM = D_MODEL // MEM_HEADS
N_EXPERTS = 32
TOP_K = 4
SWIGLU_LIMIT = 7.0
SWIGLU_ALPHA = 1.702
LANES = 128
SUBLANES = 8
VMEM_LIMIT = 48 * 1024 * 1024

TOKEN_TILE = 512
ATTN_Q_TILE = 256
ATTN_K_TILE = 512
GDN_GROUP = 4
ROUTE_TILE = 256
EXPERT_TILE = 512
ROW_UNROLL = 32


def _params(*sem):
    return pltpu.CompilerParams(dimension_semantics=sem, vmem_limit_bytes=VMEM_LIMIT)


def _dot(a, b):
    return jnp.dot(a, b, preferred_element_type=F32)


def _dot_nt(a, b):
    return lax.dot_general(a, b, (((1,), (1,)), ((), ())), preferred_element_type=F32)


def _dot_tn(a, b):
    return lax.dot_general(a, b, (((0,), (0,)), ((), ())), preferred_element_type=F32)


def _sigmoid(x):
    return 1.0 / (1.0 + jnp.exp(-x))


def _layer_norm(r, g, b):
    mu = jnp.mean(r, -1, keepdims=True)
    d = r - mu
    var = jnp.mean(d * d, -1, keepdims=True)
    return d * lax.rsqrt(var + 1e-5) * g + b


def _rope_kernel(pos_ref, inv_ref, cos_ref, sa_ref, sb_ref):
    ang = pos_ref[...] * inv_ref[...]
    sin = jnp.sin(ang)
    first_half = lax.broadcasted_iota(I32, ang.shape, 1) % DIFF_HEAD_DIM < ROT_HALF
    cos_ref[...] = jnp.cos(ang)
    sa_ref[...] = jnp.where(first_half, -sin, 0.0)
    sb_ref[...] = jnp.where(first_half, 0.0, sin)


def _rope_tables(positions):
    T = positions.size
    inv_freq = jnp.power(ROPE_THETA, -jnp.arange(0, ROT_DIM, 2, dtype=F32) / ROT_DIM)
    inv = jnp.concatenate([inv_freq, inv_freq, jnp.zeros((DIFF_HEAD_DIM - ROT_DIM,), F32)])
    inv = jnp.tile(inv, LANES // DIFF_HEAD_DIM).reshape(1, LANES)
    pos = jnp.broadcast_to(positions.reshape(T, 1).astype(F32), (T, LANES))
    rb = min(T, 1024)
    spec = pl.BlockSpec((rb, LANES), lambda i: (i, 0))
    return pl.pallas_call(
        _rope_kernel,
        grid=(T // rb,),
        in_specs=[spec, _const_spec((1, LANES))],
        out_specs=[spec] * 3,
        out_shape=[jax.ShapeDtypeStruct((T, LANES), F32)] * 3,
        compiler_params=_params("arbitrary"),
    )(pos, inv)


def _proj_diff_kernel(x_ref, w_ref, wvt_ref, cos_ref, sa_ref, sb_ref, q_ref, k_ref, vt_ref):
    xb = x_ref[...].astype(BF16)
    cos_t, sin_a, sin_b = cos_ref[...], sa_ref[...], sb_ref[...]
    qk_w = DIFF_HEADS * 2 * DIFF_HEAD_DIM
    for part, o_ref in enumerate((q_ref, k_ref)):
        acc = _dot(xb, w_ref[:, part * qk_w:(part + 1) * qk_w])
        for h in range(DIFF_HEADS):
            t = acc[:, h * LANES:(h + 1) * LANES]
            r = (t * cos_t + pltpu.roll(t, LANES - ROT_HALF, 1) * sin_a
                 + pltpu.roll(t, ROT_HALF, 1) * sin_b)
            if part == 0:
                r = r * (DIFF_HEAD_DIM ** -0.5 * math.log2(math.e))
            o_ref[:, h * LANES:(h + 1) * LANES] = r.astype(BF16)
    vt = _dot_nt(wvt_ref[...], xb)
    tk = vt_ref.shape[-1]
    for h in range(DIFF_HEADS):
        for c in range(vt_ref.shape[1]):
            vt_ref[h, c] = vt[h * LANES:(h + 1) * LANES, c * tk:(c + 1) * tk].astype(BF16)


def _proj_plain_kernel(x_ref, w_ref, o_ref):
    o_ref[...] = _dot(x_ref[...].astype(BF16), w_ref[...]).astype(BF16)


def _proj_gdn_kernel(x_ref, w_ref, cw_ref, q_ref, k_ref, v_ref, tail_ref, *, tiles_per_seq):
    xb = x_ref[...].astype(BF16)
    tm = xb.shape[0]
    d = q_ref.shape[1]
    first = pl.program_id(0) % tiles_per_seq == 0
    row8 = lax.broadcasted_iota(I32, (SUBLANES, d), 0)
    for part, o_ref in enumerate((q_ref, k_ref, v_ref)):
        cols = slice(part * d, (part + 1) * d)
        acc = _dot(xb, w_ref[:, cols])
        prev = jnp.where(first, 0.0, tail_ref[:, cols])
        tail_ref[:, cols] = acc[tm - SUBLANES:, :]
        y = acc * cw_ref[CONV_WIDTH - 1:CONV_WIDTH, cols]
        head = y[:SUBLANES]
        for j in range(1, CONV_WIDTH):
            cw_j = cw_ref[CONV_WIDTH - 1 - j:CONV_WIDTH - j, cols]
            shifted = pltpu.roll(acc, j, 0)
            y = y + shifted * cw_j
            head = head + jnp.where(row8 >= j, shifted[:SUBLANES], pltpu.roll(prev, j, 0)) * cw_j
        y = jnp.concatenate([head, y[SUBLANES:]], 0)
        y = y * _sigmoid(y)
        if part == 2:
            o_ref[...] = y.astype(BF16)
            continue
        for h in range(d // GDN_HEAD_DIM):
            lanes = slice(h * GDN_HEAD_DIM, (h + 1) * GDN_HEAD_DIM)
            yh = y[:, lanes]
            yh = yh * lax.rsqrt(jnp.sum(yh * yh, -1, keepdims=True) + 1e-6)
            if part == 0:
                yh = yh * (GDN_HEAD_DIM ** -0.5)
            o_ref[:, lanes] = yh.astype(BF16)


def _proj_gate_kernel(x_ref, w_ref, wba_ref, alog_ref, dtb_ref, sz_ref, ga_ref, gb_ref, col_ref):
    xb = x_ref[...].astype(BF16)
    z = _dot(xb, w_ref[:, :D_MODEL])
    sz_ref[...] = (z * _sigmoid(z)).astype(BF16)
    ga_ref[...] = _sigmoid(_dot(xb, w_ref[:, D_MODEL:2 * D_MODEL])).astype(BF16)
    gb_ref[...] = _sigmoid(_dot(xb, w_ref[:, 2 * D_MODEL:])).astype(BF16)

    ba = _dot(xb, wba_ref[...])
    lane = lax.broadcasted_iota(I32, ba.shape, 1)
    row = lax.broadcasted_iota(I32, ba.shape, 0)
    s = ba + dtb_ref[...]
    softplus = jnp.maximum(s, 0.0) + jnp.log1p(jnp.exp(-jnp.abs(s)))
    g = -jnp.exp(alog_ref[...]) * softplus
    shift = 1
    while shift < GDN_CHUNK:
        g = g + jnp.where(row % GDN_CHUNK >= shift, pltpu.roll(g, shift, 0), 0.0)
        shift *= 2
    col_ref[...] = jnp.where(lane < GDN_HEADS, _sigmoid(ba), g)


def _token_spec(tm, width):
    return pl.BlockSpec((tm, width), lambda i: (i, 0))


def _const_spec(shape):
    return pl.BlockSpec(shape, lambda *_: (0,) * len(shape))


def _input_projections(xf, w_in, rope, conv_w, a_log, dt_bias, B, S):
    T = xf.shape[0]
    tm = min(TOKEN_TILE, S)
    grid = (T // tm,)
    d = D_MODEL
    w_diff = w_in[:, :2 * d].astype(BF16)
    w_vt = w_in[:, 2 * d:3 * d].T.astype(BF16)
    w_gdn = w_in[:, 3 * d:6 * d].astype(BF16)
    w_z = w_in[:, 6 * d:7 * d]
    w_ba = w_in[:, 7 * d:7 * d + 2 * GDN_HEADS].astype(BF16)
    w_gate = jnp.concatenate([w_z, w_in[:, 7 * d + 2 * GDN_HEADS:]], 1).astype(BF16)
    x_spec = _token_spec(tm, d)
    act = jax.ShapeDtypeStruct((T, d), BF16)

    tk = min(ATTN_K_TILE, S)
    nt = S // tm
    dq, dk, dvt = pl.pallas_call(
        _proj_diff_kernel, grid=grid,
        in_specs=[x_spec, _const_spec((d, 2 * d)), _const_spec((d, d))] + [_token_spec(tm, LANES)] * 3,
        out_specs=[_token_spec(tm, d), _token_spec(tm, d),
                   pl.BlockSpec((None, DIFF_HEADS, tm // tk, LANES, tk),
                                lambda i: (i // nt, 0, i % nt, 0, 0))],
        out_shape=[act, act, jax.ShapeDtypeStruct((B, DIFF_HEADS, S // tk, LANES, tk), BF16)],
        compiler_params=_params("arbitrary"),
    )(xf, w_diff, w_vt, *rope)

    gq, gk, gv = pl.pallas_call(
        functools.partial(_proj_gdn_kernel, tiles_per_seq=nt), grid=grid,
        in_specs=[x_spec, _const_spec((d, 3 * d)), _const_spec((CONV_WIDTH, 3 * d))],
        out_specs=[_token_spec(tm, d)] * 3, out_shape=[act] * 3,
        scratch_shapes=[pltpu.VMEM((SUBLANES, 3 * d), F32)],
        compiler_params=_params("arbitrary"),
    )(xf, w_gdn, conv_w)

    nba = 2 * GDN_HEADS
    zeros = jnp.zeros((GDN_HEADS,), F32)
    sz, sga, sgb, col = pl.pallas_call(
        _proj_gate_kernel, grid=grid,
        in_specs=[x_spec, _const_spec((d, 3 * d)), _const_spec((d, nba)),
                  _const_spec((1, nba)), _const_spec((1, nba))],
        out_specs=[_token_spec(tm, d)] * 3 + [_token_spec(tm, nba)],
        out_shape=[act] * 3 + [jax.ShapeDtypeStruct((T, nba), F32)],
        compiler_params=_params("arbitrary"),
    )(xf, w_gate, w_ba, jnp.concatenate([zeros, a_log]).reshape(1, nba),
      jnp.concatenate([zeros, dt_bias]).reshape(1, nba))
    gdn_qkv = tuple(t.reshape(B, S, d) for t in (gq, gk, gv))
    return dq, dk, dvt, gdn_qkv, sz, sga, sgb, col


def _diff_attn_kernel(lam_ref, g_ref, q_ref, k_ref, vt_ref, o_ref,
                      sa_ref, sb_ref, m_ref, l_ref, acc_ref, *, lambda_init, tq):
    tk = vt_ref.shape[-1]
    lv = lam_ref[...]
    lam = (jnp.exp(jnp.sum(lv[0:1] * lv[1:2], -1, keepdims=True))
           - jnp.exp(jnp.sum(lv[2:3] * lv[3:4], -1, keepdims=True)) + lambda_init)
    lane = lax.broadcasted_iota(I32, (tq, LANES), 1)
    key = lax.broadcasted_iota(I32, (tk, 2 * tq), 0)
    qry = lax.broadcasted_iota(I32, (tk, 2 * tq), 1)
    qry = jnp.where(qry >= tq, qry - tq, qry)
    bufs = (sa_ref, sb_ref)

    for qi in range(q_ref.shape[0] // tq):
        q = q_ref[qi * tq:(qi + 1) * tq, :]
        zero = jnp.zeros_like(q)
        q2 = jnp.concatenate([jnp.where(lane < DIFF_HEAD_DIM, q, zero),
                              jnp.where(lane >= DIFF_HEAD_DIM, q, zero)], 0)
        last = (qi * tq) // tk

        def scores(j):
            return _dot_nt(k_ref[j * tk:(j + 1) * tk, :], q2)

        def update(s, j):
            m = m_ref[...]
            m_new = jnp.maximum(m, jnp.max(s, 0, keepdims=True))
            alpha = jnp.exp2(m - m_new)
            p = jnp.exp2(s - m_new)
            l_ref[...] = alpha * l_ref[...] + jnp.sum(p, 0, keepdims=True)
            acc_ref[...] = alpha * acc_ref[...] + _dot(vt_ref[j], p.astype(BF16))
            m_ref[...] = m_new

        m_ref[...] = jnp.full(m_ref.shape, -jnp.inf, F32)
        l_ref[...] = jnp.zeros_like(l_ref)
        acc_ref[...] = jnp.zeros_like(acc_ref)
        bufs[0][...] = scores(0)
        for j in range(last + 1):
            if j < last:
                bufs[(j + 1) % 2][...] = scores(j + 1)
            s = bufs[j % 2][...]
            if j == last:
                s = jnp.where(key + last * tk <= qry + qi * tq, s, -jnp.inf)
            update(s, j)

        o = acc_ref[...] / l_ref[...]
        o = o[:, :tq] - lam * o[:, tq:]
        o = o * lax.rsqrt(jnp.mean(o * o, 0, keepdims=True) + 1e-6) * g_ref[...]
        o_ref[qi * tq:(qi + 1) * tq, :] = (o * (1.0 - lambda_init)).T.astype(BF16)


def _diff_attention(dq, dk, dvt, lam_vecs, subln_g, B, S, lambda_init):
    tq = min(ATTN_Q_TILE, S)
    tk = dvt.shape[-1]
    d = D_MODEL
    q3, k3 = dq.reshape(B, S, d), dk.reshape(B, S, d)
    seq = pl.BlockSpec((None, S, LANES), lambda b, h: (b, 0, h))
    out = pl.pallas_call(
        functools.partial(_diff_attn_kernel, lambda_init=lambda_init, tq=tq),
        grid=(B, DIFF_HEADS),
        in_specs=[_const_spec((4, DIFF_HEAD_DIM)), _const_spec((LANES, 1)), seq, seq,
                  pl.BlockSpec((None, None, S // tk, LANES, tk), lambda b, h: (b, h, 0, 0, 0))],
        out_specs=seq,
        out_shape=jax.ShapeDtypeStruct((B, S, d), BF16),
        scratch_shapes=[pltpu.VMEM((tk, 2 * tq), F32), pltpu.VMEM((tk, 2 * tq), F32),
                        pltpu.VMEM((1, 2 * tq), F32), pltpu.VMEM((1, 2 * tq), F32),
                        pltpu.VMEM((LANES, 2 * tq), F32)],
        compiler_params=_params("arbitrary", "arbitrary"),
    )(lam_vecs, subln_g.reshape(LANES, 1), q3, k3, dvt)
    return out.reshape(B * S, d)


def _gdn_local_kernel(q_ref, k_ref, v_ref, col_ref, row_ref,
                      u_ref, w_ref, qd_ref, kd_ref, qk_ref, gl_ref):
    C = GDN_CHUNK
    R = q_ref.shape[0]
    ri = lax.broadcasted_iota(I32, (R, R), 0)
    ci = lax.broadcasted_iota(I32, (R, R), 1)
    same = (ri // C) == (ci // C)
    incl = same & (ri >= ci)
    strict = same & (ri > ci)
    eye = (ri == ci).astype(F32)
    heads = range(GDN_HEADS)
    head_lanes = [slice(h * GDN_HEAD_DIM, (h + 1) * GDN_HEAD_DIM) for h in heads]
    decays, pbs, t_mats = [], [], []
    for h in heads:
        kh = k_ref[:, head_lanes[h]]
        g_c = col_ref[:, GDN_HEADS + h:GDN_HEADS + h + 1]
        g_r = row_ref[GDN_HEADS + h:GDN_HEADS + h + 1, :]
        decay = jnp.where(incl, jnp.exp(jnp.where(incl, g_c - g_r, 0.0)), 0.0)
        p = -(jnp.where(strict, _dot_nt(kh, kh) * decay, 0.0) * col_ref[:, h:h + 1])
        decays.append(decay)
        pbs.append(p.astype(BF16))
        t_mats.append(eye + p)
    for _ in range(5):
        pbs = [_dot(pb, pb).astype(BF16) for pb in pbs]
        t_mats = [t + _dot(t.astype(BF16), pb) for t, pb in zip(t_mats, pbs)]
    for h in heads:
        lanes = head_lanes[h]
        qh, kh, vh = q_ref[:, lanes], k_ref[:, lanes], v_ref[:, lanes]
        beta_c = col_ref[:, h:h + 1]
        g_c = col_ref[:, GDN_HEADS + h:GDN_HEADS + h + 1]
        decay = decays[h]
        eg_c = jnp.exp(g_c)
        kf = kh.astype(F32)
        rhs = jnp.concatenate([vh.astype(F32) * beta_c, kf * (beta_c * eg_c)], 1)
        uw = _dot(t_mats[h].astype(BF16), rhs.astype(BF16))
        u_ref[:, lanes] = uw[:, :GDN_HEAD_DIM].astype(BF16)
        w_ref[:, lanes] = uw[:, GDN_HEAD_DIM:].astype(BF16)
        qd_ref[:, lanes] = (qh.astype(F32) * eg_c).astype(BF16)
        for c in range(R // C):
            rows = slice(c * C, (c + 1) * C)
            g_last = g_c[(c + 1) * C - 1:(c + 1) * C, :]
            kd_ref[rows, lanes] = (kf[rows] * jnp.exp(g_last - g_c[rows])).astype(BF16)
            gl_ref[c, h:h + 1, :] = jnp.broadcast_to(jnp.exp(g_last), (1, GDN_HEAD_DIM))
            qk_ref[c, h] = (_dot_nt(qh[rows], kh[rows]) * decay[rows, rows]).astype(BF16)


def _gdn_scan_kernel(u_ref, w_ref, qd_ref, kd_ref, qk_ref, gl_ref, sz_ref, ng_ref, o_ref, state_ref):
    @pl.when(pl.program_id(1) == 0)
    def _():
        state_ref[...] = jnp.zeros_like(state_ref)

    C = GDN_CHUNK
    ng = ng_ref[...]
    heads = range(GDN_HEADS)
    lanes = [slice(h * GDN_HEAD_DIM, (h + 1) * GDN_HEAD_DIM) for h in heads]
    for c in range(u_ref.shape[0] // C):
        rows = slice(c * C, (c + 1) * C)
        st = [state_ref[h] for h in heads]
        ws = [_dot(jnp.concatenate([w_ref[rows, lanes[h]], qd_ref[rows, lanes[h]]], 0),
                   st[h].astype(BF16)) for h in heads]
        vnb = [(u_ref[rows, lanes[h]].astype(F32) - ws[h][:C]).astype(BF16) for h in heads]
        for h in heads:
            state_ref[h] = st[h] * gl_ref[c, h:h + 1, :] + _dot_tn(kd_ref[rows, lanes[h]], vnb[h])
        for h in heads:
            o = ws[h][C:] + _dot(qk_ref[c, h], vnb[h])
            o = o * lax.rsqrt(jnp.mean(o * o, -1, keepdims=True) + 1e-6) * ng
            o_ref[rows, lanes[h]] = (o * sz_ref[rows, lanes[h]].astype(F32)).astype(BF16)


def _gated_deltanet(gq, gk, gv, sz, col, norm_g, B, S):
    d = D_MODEL
    C = GDN_CHUNK
    R = min(GDN_GROUP * C, S)
    nc = R // C
    col_arr = col.reshape(B, S, 2 * GDN_HEADS)
    row_arr = col_arr.transpose(0, 2, 1)
    blk = pl.BlockSpec((None, R, d), lambda b, i: (b, i, 0))
    qk_blk = pl.BlockSpec((None, nc, GDN_HEADS, C, C), lambda b, i: (b, i, 0, 0, 0))
    gl_blk = pl.BlockSpec((None, nc, GDN_HEADS, GDN_HEAD_DIM), lambda b, i: (b, i, 0, 0))
    act = jax.ShapeDtypeStruct((B, S, d), BF16)
    u, w, qd, kd, qk, gl = pl.pallas_call(
        _gdn_local_kernel,
        grid=(B, S // R),
        in_specs=[blk, blk, blk,
                  pl.BlockSpec((None, R, 2 * GDN_HEADS), lambda b, i: (b, i, 0)),
                  pl.BlockSpec((None, 2 * GDN_HEADS, R), lambda b, i: (b, 0, i))],
        out_specs=[blk, blk, blk, blk, qk_blk, gl_blk],
        out_shape=[act, act, act, act,
                   jax.ShapeDtypeStruct((B, S // C, GDN_HEADS, C, C), BF16),
                   jax.ShapeDtypeStruct((B, S // C, GDN_HEADS, GDN_HEAD_DIM), F32)],
        compiler_params=_params("arbitrary", "arbitrary"),
    )(gq, gk, gv, col_arr, row_arr)
    out = pl.pallas_call(
        _gdn_scan_kernel,
        grid=(B, S // R),
        in_specs=[blk, blk, blk, blk, qk_blk, gl_blk, blk, _const_spec((1, GDN_HEAD_DIM))],
        out_specs=blk,
        out_shape=act,
        scratch_shapes=[pltpu.VMEM((GDN_HEADS, GDN_HEAD_DIM, GDN_HEAD_DIM), F32)],
        compiler_params=_params("arbitrary", "arbitrary"),
    )(u, w, qd, kd, qk, gl, sz.reshape(B, S, d), norm_g.reshape(1, GDN_HEAD_DIM))
    return out.reshape(B * S, d)


def _mix_kernel(x_ref, od_ref, og_ref, ga_ref, gb_ref, wd_ref, wg_ref, wm_ref, g_ref, b_ref,
                o_ref, *, alpha):
    yd = _dot(od_ref[...], wd_ref[...])
    yg = _dot(og_ref[...], wg_ref[...])
    m = ga_ref[...].astype(F32) * yd + gb_ref[...].astype(F32) * yg
    r = alpha * x_ref[...] + _dot(m.astype(BF16), wm_ref[...])
    o_ref[...] = _layer_norm(r, g_ref[...], b_ref[...])


def _mix(xf, od, og, sga, sgb, w_diff_o, w_gdn_o, w_mix_o, ln_g, ln_b, alpha):
    T, d = xf.shape
    tm = min(TOKEN_TILE, T)
    tok = _token_spec(tm, d)
    wspec = _const_spec((d, d))
    vec = _const_spec((1, d))
    return pl.pallas_call(
        functools.partial(_mix_kernel, alpha=alpha), grid=(T // tm,),
        in_specs=[tok] * 5 + [wspec] * 3 + [vec] * 2,
        out_specs=tok, out_shape=jax.ShapeDtypeStruct((T, d), F32),
        compiler_params=_params("arbitrary"),
    )(xf, od, og, sga, sgb, w_diff_o.astype(BF16), w_gdn_o.astype(BF16), w_mix_o.astype(BF16),
      ln_g.reshape(1, d), ln_b.reshape(1, d))


def _cross_kernel(x_ref, k_ref, v_ref, wq_ref, wo_ref, g_ref, b_ref, wr_ref, br_ref,
                  o_ref, idx_ref, gate_ref, *, alpha):
    half = x_ref.shape[0] // 2
    for part in range(2):
        rows = slice(part * half, (part + 1) * half)
        x = x_ref[rows, :]
        q = (_dot(x.astype(BF16), wq_ref[...]) * (MEM_HEAD_DIM ** -0.5)).astype(BF16)
        outs = []
        for h in range(MEM_HEADS):
            lanes = slice(h * MEM_HEAD_DIM, (h + 1) * MEM_HEAD_DIM)
            s = _dot_nt(q[:, lanes], k_ref[:, lanes])
            p = jnp.exp(s - jnp.max(s, -1, keepdims=True))
            outs.append(_dot(p.astype(BF16), v_ref[:, lanes]) / jnp.sum(p, -1, keepdims=True))
        o = jnp.concatenate(outs, 1).astype(BF16)
        r = alpha * x + _dot(o, wo_ref[...])
        x2 = _layer_norm(r, g_ref[...], b_ref[...])
        o_ref[rows, :] = x2

        logits = _dot(x2.astype(BF16), wr_ref[...]) + br_ref[...]
        lane = lax.broadcasted_iota(I32, logits.shape, 1)
        vals, idxs = [], []
        for _ in range(TOP_K):
            mx = jnp.max(logits, -1, keepdims=True)
            ix = jnp.min(jnp.where(logits == mx, lane, N_EXPERTS), -1, keepdims=True)
            vals.append(mx)
            idxs.append(ix)
            logits = jnp.where(lane == ix, -jnp.inf, logits)
        e = jnp.exp(jnp.concatenate(vals, 1) - vals[0])
        gate_ref[rows, :] = e / jnp.sum(e, -1, keepdims=True)
        idx_ref[rows, :] = jnp.concatenate(idxs, 1)


def _cross_attention(x1, mem, w_cq, w_ck, w_cv, w_co, ln_g, ln_b, w_router, b_router, B, S, alpha):
    T, d = x1.shape
    M = mem.shape[1]
    w_kv = jnp.concatenate([w_ck, w_cv], 1).astype(BF16)
    tmem = min(TOKEN_TILE, B * M)
    kv = pl.pallas_call(
        _proj_plain_kernel, grid=(B * M // tmem,),
        in_specs=[_token_spec(tmem, d), _const_spec((d, 2 * d))],
        out_specs=_token_spec(tmem, 2 * d),
        out_shape=jax.ShapeDtypeStruct((B * M, 2 * d), BF16),
        compiler_params=_params("arbitrary"),
    )(mem.reshape(B * M, d), w_kv).reshape(B, M, 2 * d)

    tm = min(TOKEN_TILE, S)
    nt = S // tm
    tok = pl.BlockSpec((tm, d), lambda b, i: (b * nt + i, 0))
    small = pl.BlockSpec((tm, TOP_K), lambda b, i: (b * nt + i, 0))
    return pl.pallas_call(
        functools.partial(_cross_kernel, alpha=alpha), grid=(B, nt),
        in_specs=[tok,
                  pl.BlockSpec((None, M, d), lambda b, i: (b, 0, 0)),
                  pl.BlockSpec((None, M, d), lambda b, i: (b, 0, 1)),
                  _const_spec((d, d)), _const_spec((d, d)), _const_spec((1, d)), _const_spec((1, d)),
                  _const_spec((d, N_EXPERTS)), _const_spec((1, N_EXPERTS))],
        out_specs=[tok, small, small],
        out_shape=[jax.ShapeDtypeStruct((T, d), F32), jax.ShapeDtypeStruct((T, TOP_K), I32),
                   jax.ShapeDtypeStruct((T, TOP_K), F32)],
        compiler_params=_params("arbitrary", "arbitrary"),
    )(x1, kv, kv, w_cq.astype(BF16), w_co.astype(BF16), ln_g.reshape(1, d), ln_b.reshape(1, d),
      w_router.astype(BF16), b_router.reshape(1, N_EXPERTS))


def _rank_kernel(idx_ref, rank_ref, cnt_ref, run_ref):
    @pl.when(pl.program_id(0) == 0)
    def _():
        run_ref[...] = jnp.zeros_like(run_ref)

    idx = idx_ref[...]
    tm = idx.shape[0]
    lane = lax.broadcasted_iota(I32, (tm, N_EXPERTS), 1)
    hits = [lane == idx[:, k:k + 1] for k in range(TOP_K)]
    onehot = sum(hit.astype(F32) for hit in hits)
    r = lax.broadcasted_iota(I32, (tm, tm), 0)
    c = lax.broadcasted_iota(I32, (tm, tm), 1)
    lower = (r > c).astype(BF16)
    before = _dot(lower, onehot.astype(BF16)) + run_ref[...]
    ranks = [jnp.sum(jnp.where(hit, before, 0.0), -1, keepdims=True) for hit in hits]
    rank_ref[...] = jnp.concatenate(ranks, 1).astype(I32)
    run_ref[...] = run_ref[...] + jnp.sum(onehot, 0, keepdims=True)
    cnt_ref[...] = run_ref[...]


def _dest_kernel(idx_ref, rank_ref, start_ref, dest_ref):
    idx = idx_ref[...]
    lane = lax.broadcasted_iota(I32, (idx.shape[0], N_EXPERTS), 1)
    starts = start_ref[...]
    cols = [jnp.sum(jnp.where(lane == idx[:, k:k + 1], starts, 0), -1, keepdims=True)
            for k in range(TOP_K)]
    dest_ref[...] = rank_ref[...] + jnp.concatenate(cols, 1)


def _row_copy(src_ref, src_row, dst_ref, dst_row, sem):
    return pltpu.make_async_copy(src_ref.at[pl.ds(src_row, 1), :], dst_ref.at[pl.ds(dst_row, 1), :], sem)


def _index_copy(dest_hbm, dest_smem, sem_idx, step, slot, n):
    return pltpu.make_async_copy(dest_hbm.at[pl.ds(pl.multiple_of(step * n, n), n)],
                                 dest_smem.at[pl.ds(pl.multiple_of(slot * n, n), n)], sem_idx.at[slot])


def _dispatch_kernel(zero_rows, dest_hbm, x_ref, xs_hbm, dest_smem, zero_buf, sem_idx, sem_rows):
    tm = x_ref.shape[0]
    n = tm * TOP_K
    i = pl.program_id(0)
    last = pl.num_programs(0) - 1
    slot = i % 2

    @pl.when(i == 0)
    def _():
        _index_copy(dest_hbm, dest_smem, sem_idx, 0, 0, n).start()
        zero_buf[...] = jnp.zeros_like(zero_buf)
        bm = zero_buf.shape[0]

        def zero_tile(t):
            row = pl.multiple_of(jnp.maximum(zero_rows[t], 0), bm)
            return pltpu.make_async_copy(zero_buf, xs_hbm.at[pl.ds(row, bm), :], sem_rows)

        for t in range(zero_rows.shape[0]):
            pl.when(zero_rows[t] >= 0)(lambda t=t: zero_tile(t).start())
        for t in range(zero_rows.shape[0]):
            pl.when(zero_rows[t] >= 0)(lambda t=t: zero_tile(t).wait())

    _index_copy(dest_hbm, dest_smem, sem_idx, i, slot, n).wait()

    @pl.when(i < last)
    def _():
        _index_copy(dest_hbm, dest_smem, sem_idx, i + 1, 1 - slot, n).start()

    base = slot * n

    def issue(jj, _):
        for u in range(ROW_UNROLL):
            j = jj * ROW_UNROLL + u
            tok = jj * (ROW_UNROLL // TOP_K) + u // TOP_K
            _row_copy(x_ref, tok, xs_hbm, dest_smem[base + j], sem_rows).start(priority=u % 2)
        return 0

    lax.fori_loop(0, n // ROW_UNROLL, issue, 0)
    for _ in range(TOP_K):
        pltpu.make_async_copy(x_ref, xs_hbm.at[pl.ds(0, tm), :], sem_rows).wait()


def _expert_kernel(te_ref, tv_ref, xs_ref, w1_ref, b1_ref, w2_ref, b2_ref, ys_ref):
    i = pl.program_id(0)

    @pl.when(tv_ref[i] != 0)
    def _():
        ff = w2_ref.shape[0]
        h = _dot(xs_ref[...].astype(BF16), w1_ref[...]) + b1_ref[...]
        h_gate = jnp.minimum(h[:, :ff], SWIGLU_LIMIT)
        h_up = jnp.clip(h[:, ff:], -SWIGLU_LIMIT, SWIGLU_LIMIT)
        act = h_gate * _sigmoid(SWIGLU_ALPHA * h_gate) * (h_up + 1.0)
        ys_ref[...] = _dot(act.astype(BF16), w2_ref[...]) + b2_ref[...]

    @pl.when(tv_ref[i] == 0)
    def _():
        ys_ref[...] = jnp.zeros_like(ys_ref)


def _combine_kernel(dest_hbm, ys_hbm, x_ref, gate_ref, g_ref, b_ref, o_ref,
                    dest_smem, buf, sem_idx, sem_rows, *, alpha):
    tm = x_ref.shape[0]
    n = tm * TOP_K
    i = pl.program_id(0)
    last = pl.num_programs(0) - 1
    slot = i % 2

    def gather_tile(step, s):
        cp = _index_copy(dest_hbm, dest_smem, sem_idx, step, s, n)
        cp.start()
        cp.wait()
        base = s * n

        def issue(jj, _):
            for u in range(ROW_UNROLL):
                j = jj * ROW_UNROLL + u
                tok = jj * (ROW_UNROLL // TOP_K) + u // TOP_K
                _row_copy(ys_hbm, dest_smem[base + j], buf.at[s, u % TOP_K], tok,
                          sem_rows.at[s]).start(priority=u % 2)
            return 0

        lax.fori_loop(0, n // ROW_UNROLL, issue, 0)

    @pl.when(i == 0)
    def _():
        gather_tile(0, 0)

    @pl.when(i < last)
    def _():
        gather_tile(i + 1, 1 - slot)

    for k in range(TOP_K):
        pltpu.make_async_copy(ys_hbm.at[pl.ds(0, tm), :], buf.at[slot, k], sem_rows.at[slot]).wait()
    gates = gate_ref[...]
    y = sum(gates[:, k:k + 1] * buf[slot, k] for k in range(TOP_K))
    o_ref[...] = _layer_norm(alpha * x_ref[...] + y, g_ref[...], b_ref[...])


def _moe(x2, idx, gates, w1, b1, w2, b2, ln_g, ln_b, alpha):
    T, d = x2.shape
    ff = w2.shape[1]
    tk = T * TOP_K
    bm = EXPERT_TILE
    n_tiles = -(-tk // bm) + N_EXPERTS
    R = n_tiles * bm

    tr = min(TOKEN_TILE, T)
    small = lambda tm: pl.BlockSpec((tm, TOP_K), lambda i: (i, 0))
    rank, counts = pl.pallas_call(
        _rank_kernel, grid=(T // tr,),
        in_specs=[small(tr)], out_specs=[small(tr), _const_spec((1, N_EXPERTS))],
        out_shape=[jax.ShapeDtypeStruct((T, TOP_K), I32), jax.ShapeDtypeStruct((1, N_EXPERTS), F32)],
        scratch_shapes=[pltpu.VMEM((1, N_EXPERTS), F32)],
        compiler_params=_params("arbitrary"),
    )(idx)

    counts = counts.reshape(N_EXPERTS).astype(I32)
    padded = (counts + bm - 1) // bm * bm
    pad_ends = jnp.cumsum(padded)
    pad_starts = pad_ends - padded
    tile_row = jnp.arange(n_tiles, dtype=I32) * bm
    tile_e = jnp.minimum(jnp.sum(pad_ends[None, :] <= tile_row[:, None], 1), N_EXPERTS - 1).astype(I32)
    tile_valid = (tile_row < pad_ends[-1]).astype(I32)

    dest = pl.pallas_call(
        _dest_kernel, grid=(T // tr,),
        in_specs=[small(tr), small(tr), _const_spec((1, N_EXPERTS))], out_specs=small(tr),
        out_shape=jax.ShapeDtypeStruct((T, TOP_K), I32),
        compiler_params=_params("arbitrary"),
    )(idx, rank, pad_starts.reshape(1, N_EXPERTS)).reshape(tk)

    tm = min(ROUTE_TILE, T)
    n = tm * TOP_K
    any_spec = pl.BlockSpec(memory_space=pl.ANY)
    last_tile = jnp.where(padded > 0, pad_ends - bm, -1)
    tail = pad_ends[-1] + jnp.arange(n_tiles - tk // bm, dtype=I32) * bm
    tail = jnp.where(tail < R, tail, -1)
    zero_rows = jnp.concatenate([last_tile, tail]).astype(I32)
    xs = pl.pallas_call(
        _dispatch_kernel,
        grid_spec=pltpu.PrefetchScalarGridSpec(
            num_scalar_prefetch=1, grid=(T // tm,),
            in_specs=[any_spec, pl.BlockSpec((tm, d), lambda i, zr: (i, 0))], out_specs=any_spec,
            scratch_shapes=[pltpu.SMEM((2 * n,), I32), pltpu.VMEM((bm, d), F32),
                            pltpu.SemaphoreType.DMA((2,)), pltpu.SemaphoreType.DMA(())]),
        out_shape=jax.ShapeDtypeStruct((R, d), F32),
        compiler_params=_params("arbitrary"),
    )(zero_rows, dest, x2)

    ys = pl.pallas_call(
        _expert_kernel,
        grid_spec=pltpu.PrefetchScalarGridSpec(
            num_scalar_prefetch=2, grid=(n_tiles,),
            in_specs=[pl.BlockSpec((bm, d), lambda i, te, tv: (i, 0)),
                      pl.BlockSpec((None, d, 2 * ff), lambda i, te, tv: (te[i], 0, 0)),
                      pl.BlockSpec((None, 1, 2 * ff), lambda i, te, tv: (te[i], 0, 0)),
                      pl.BlockSpec((None, ff, d), lambda i, te, tv: (te[i], 0, 0)),
                      pl.BlockSpec((None, 1, d), lambda i, te, tv: (te[i], 0, 0))],
            out_specs=pl.BlockSpec((bm, d), lambda i, te, tv: (i, 0))),
        out_shape=jax.ShapeDtypeStruct((R, d), F32),
        compiler_params=_params("arbitrary"),
    )(tile_e, tile_valid, xs, w1.astype(BF16), b1.reshape(N_EXPERTS, 1, 2 * ff),
      w2.astype(BF16), b2.reshape(N_EXPERTS, 1, d))

    return pl.pallas_call(
        functools.partial(_combine_kernel, alpha=alpha), grid=(T // tm,),
        in_specs=[any_spec, any_spec, _token_spec(tm, d), small(tm), _const_spec((1, d)), _const_spec((1, d))],
        out_specs=_token_spec(tm, d),
        out_shape=jax.ShapeDtypeStruct((T, d), F32),
        scratch_shapes=[pltpu.SMEM((2 * n,), I32), pltpu.VMEM((2, TOP_K, tm, d), F32),
                        pltpu.SemaphoreType.DMA((2,)), pltpu.SemaphoreType.DMA((2,))],
        compiler_params=_params("arbitrary"),
    )(dest, ys, x2, gates, ln_g.reshape(1, d), ln_b.reshape(1, d))


def kernel(x, mem, positions, w_in, diff_lambda_q1, diff_lambda_k1, diff_lambda_q2, diff_lambda_k2, diff_subln_g, w_diff_o, gdn_conv_w, gdn_A_log, gdn_dt_bias, gdn_norm_g, w_gdn_o, w_mix_o, ln1_g, ln1_b, w_cq, w_ck, w_cv, w_co, ln2_g, ln2_b, w_router, b_router, w_exp_in, b_exp_in, w_exp_out, b_exp_out, ln3_g, ln3_b):
    B, S, d = x.shape
    depth = w_in.shape[0]
    alpha = (2 * depth) ** 0.25
    rope = _rope_tables(positions)
    xf = x.reshape(B * S, d)
    for l in range(depth):
        lambda_init = 0.8 - 0.6 * math.exp(-0.3 * l)
        dq, dk, dvt, (gq, gk, gv), sz, sga, sgb, gdn_col = _input_projections(
            xf, w_in[l], rope, gdn_conv_w[l], gdn_A_log[l], gdn_dt_bias[l], B, S)
        lam_vecs = jnp.stack([diff_lambda_q1[l], diff_lambda_k1[l], diff_lambda_q2[l], diff_lambda_k2[l]])
        od = _diff_attention(dq, dk, dvt, lam_vecs, diff_subln_g[l], B, S, lambda_init)
        og = _gated_deltanet(gq, gk, gv, sz, gdn_col, gdn_norm_g[l], B, S)
        x1 = _mix(xf, od, og, sga, sgb, w_diff_o[l], w_gdn_o[l], w_mix_o[l], ln1_g[l], ln1_b[l], alpha)
        x2, idx, gates = _cross_attention(x1, mem, w_cq[l], w_ck[l], w_cv[l], w_co[l], ln2_g[l], ln2_b[l],
                                          w_router[l], b_router[l], B, S, alpha)
        xf = _moe(x2, idx, gates, w_exp_in[l], b_exp_in[l], w_exp_out[l], b_exp_out[l],
                  ln3_g[l], ln3_b[l], alpha)
    return xf.reshape(B, S, d)
```

```python
import functools
import math

import jax
import jax.numpy as jnp
from jax import lax
from jax.experimental import pallas as pl
from jax.experimental.pallas import tpu as pltpu

F32 = jnp.float32
BF16 = jnp.bfloat16
I32 = jnp.int32

D_MODEL = 1024
DIFF_HEADS = 8
DIFF_HEAD_DIM = 64
ROPE_THETA = 500000.0
ROT_DIM = DIFF_HEAD_DIM // 4
ROT_HALF = ROT_DIM // 2
GDN_HEADS = 8
GDN_HEAD_DIM = 128
CONV_WIDTH = 4
GDN_CHUNK = 64
MEM_HEADS = 4
MEM_HEAD_DIM = D_MODEL // MEM_HEADS
N_EXPERTS = 32
TOP_K = 4
SWIGLU_LIMIT = 7.0
SWIGLU_ALPHA = 1.702
LANES = 128
SUBLANES = 8
VMEM_LIMIT = 48 * 1024 * 1024

TOKEN_TILE = 1024
ATTN_Q_TILE = 512
ATTN_K_TILE = 512
GDN_GROUP = 4
ROUTE_TILE = 512
EXPERT_TILE = 512
ROW_UNROLL = 32


def _params(*sem):
    return pltpu.CompilerParams(dimension_semantics=sem, vmem_limit_bytes=VMEM_LIMIT)


def _dot(a, b):
    return jnp.dot(a, b, preferred_element_type=F32)


def _dot_nt(a, b):
    return lax.dot_general(a, b, (((1,), (1,)), ((), ())), preferred_element_type=F32)


def _dot_tn(a, b):
    return lax.dot_general(a, b, (((0,), (0,)), ((), ())), preferred_element_type=F32)


def _sigmoid(x):
    return 1.0 / (1.0 + jnp.exp(-x))


def _layer_norm(r, g, b):
    mu = jnp.mean(r, -1, keepdims=True)
    d = r - mu
    var = jnp.mean(d * d, -1, keepdims=True)
    return d * lax.rsqrt(var + 1e-5) * g + b


def _rope_kernel(pos_ref, inv_ref, cos_ref, sa_ref, sb_ref):
    ang = pos_ref[...] * inv_ref[...]
    sin = jnp.sin(ang)
    first_half = lax.broadcasted_iota(I32, ang.shape, 1) % DIFF_HEAD_DIM < ROT_HALF
    cos_ref[...] = jnp.cos(ang)
    sa_ref[...] = jnp.where(first_half, -sin, 0.0)
    sb_ref[...] = jnp.where(first_half, 0.0, sin)


def _rope_tables(positions):
    T = positions.size
    inv_freq = jnp.power(ROPE_THETA, -jnp.arange(0, ROT_DIM, 2, dtype=F32) / ROT_DIM)
    inv = jnp.concatenate([inv_freq, inv_freq, jnp.zeros((DIFF_HEAD_DIM - ROT_DIM,), F32)])
    inv = jnp.tile(inv, LANES // DIFF_HEAD_DIM).reshape(1, LANES)
    pos = jnp.broadcast_to(positions.reshape(T, 1).astype(F32), (T, LANES))
    rb = min(T, 1024)
    spec = pl.BlockSpec((rb, LANES), lambda i: (i, 0))
    return pl.pallas_call(
        _rope_kernel,
        grid=(T // rb,),
        in_specs=[spec, _const_spec((1, LANES))],
        out_specs=[spec] * 3,
        out_shape=[jax.ShapeDtypeStruct((T, LANES), F32)] * 3,
        compiler_params=_params("arbitrary"),
    )(pos, inv)


def _proj_diff_kernel(x_ref, w_ref, wvt_ref, cos_ref, sa_ref, sb_ref, q_ref, k_ref, vt_ref):
    xb = x_ref[...].astype(BF16)
    cos_t, sin_a, sin_b = cos_ref[...], sa_ref[...], sb_ref[...]
    qk_w = DIFF_HEADS * 2 * DIFF_HEAD_DIM
    for part, o_ref in enumerate((q_ref, k_ref)):
        acc = _dot(xb, w_ref[:, part * qk_w:(part + 1) * qk_w])
        for h in range(DIFF_HEADS):
            t = acc[:, h * LANES:(h + 1) * LANES]
            r = (t * cos_t + pltpu.roll(t, LANES - ROT_HALF, 1) * sin_a
                 + pltpu.roll(t, ROT_HALF, 1) * sin_b)
            if part == 0:
                r = r * (DIFF_HEAD_DIM ** -0.5 * math.log2(math.e))
            o_ref[:, h * LANES:(h + 1) * LANES] = r.astype(BF16)
    vt = _dot_nt(wvt_ref[...], xb)
    tk = vt_ref.shape[-1]
    for h in range(DIFF_HEADS):
        for c in range(vt_ref.shape[1]):
            vt_ref[h, c] = vt[h * LANES:(h + 1) * LANES, c * tk:(c + 1) * tk].astype(BF16)


def _proj_plain_kernel(x_ref, w_ref, o_ref):
    o_ref[...] = _dot(x_ref[...].astype(BF16), w_ref[...]).astype(BF16)


def _proj_gdn_kernel(x_ref, w_ref, cw_ref, q_ref, k_ref, v_ref, tail_ref, *, tiles_per_seq):
    xb = x_ref[...].astype(BF16)
    tm = xb.shape[0]
    d = q_ref.shape[1]
    first = pl.program_id(0) % tiles_per_seq == 0
    row8 = lax.broadcasted_iota(I32, (SUBLANES, d), 0)
    for part, o_ref in enumerate((q_ref, k_ref, v_ref)):
        cols = slice(part * d, (part + 1) * d)
        acc = _dot(xb, w_ref[:, cols])
        prev = jnp.where(first, 0.0, tail_ref[:, cols])
        tail_ref[:, cols] = acc[tm - SUBLANES:, :]
        y = acc * cw_ref[CONV_WIDTH - 1:CONV_WIDTH, cols]
        head = y[:SUBLANES]
        for j in range(1, CONV_WIDTH):
            cw_j = cw_ref[CONV_WIDTH - 1 - j:CONV_WIDTH - j, cols]
            shifted = pltpu.roll(acc, j, 0)
            y = y + shifted * cw_j
            head = head + jnp.where(row8 >= j, shifted[:SUBLANES], pltpu.roll(prev, j, 0)) * cw_j
        y = jnp.concatenate([head, y[SUBLANES:]], 0)
        y = y * _sigmoid(y)
        if part == 2:
            o_ref[...] = y.astype(BF16)
            continue
        for h in range(d // GDN_HEAD_DIM):
            lanes = slice(h * GDN_HEAD_DIM, (h + 1) * GDN_HEAD_DIM)
            yh = y[:, lanes]
            yh = yh * lax.rsqrt(jnp.sum(yh * yh, -1, keepdims=True) + 1e-6)
            if part == 0:
                yh = yh * (GDN_HEAD_DIM ** -0.5)
            o_ref[:, lanes] = yh.astype(BF16)


def _proj_gate_kernel(x_ref, w_ref, wba_ref, alog_ref, dtb_ref, sz_ref, ga_ref, gb_ref, col_ref):
    xb = x_ref[...].astype(BF16)
    z = _dot(xb, w_ref[:, :D_MODEL])
    sz_ref[...] = (z * _sigmoid(z)).astype(BF16)
    ga_ref[...] = _sigmoid(_dot(xb, w_ref[:, D_MODEL:2 * D_MODEL])).astype(BF16)
    gb_ref[...] = _sigmoid(_dot(xb, w_ref[:, 2 * D_MODEL:])).astype(BF16)

    ba = _dot(xb, wba_ref[...])
    lane = lax.broadcasted_iota(I32, ba.shape, 1)
    row = lax.broadcasted_iota(I32, ba.shape, 0)
    s = ba + dtb_ref[...]
    softplus = jnp.maximum(s, 0.0) + jnp.log1p(jnp.exp(-jnp.abs(s)))
    g = -jnp.exp(alog_ref[...]) * softplus
    shift = 1
    while shift < GDN_CHUNK:
        g = g + jnp.where(row % GDN_CHUNK >= shift, pltpu.roll(g, shift, 0), 0.0)
        shift *= 2
    col_ref[...] = jnp.where(lane < GDN_HEADS, _sigmoid(ba), g)


def _token_spec(tm, width):
    return pl.BlockSpec((tm, width), lambda i: (i, 0))


def _const_spec(shape):
    return pl.BlockSpec(shape, lambda *_: (0,) * len(shape))


def _input_projections(xf, w_in, rope, conv_w, a_log, dt_bias, B, S):
    T = xf.shape[0]
    tm = min(TOKEN_TILE, S)
    grid = (T // tm,)
    d = D_MODEL
    w_diff = w_in[:, :2 * d].astype(BF16)
    w_vt = w_in[:, 2 * d:3 * d].T.astype(BF16)
    w_gdn = w_in[:, 3 * d:6 * d].astype(BF16)
    w_z = w_in[:, 6 * d:7 * d]
    w_ba = w_in[:, 7 * d:7 * d + 2 * GDN_HEADS].astype(BF16)
    w_gate = jnp.concatenate([w_z, w_in[:, 7 * d + 2 * GDN_HEADS:]], 1).astype(BF16)
    x_spec = _token_spec(tm, d)
    act = jax.ShapeDtypeStruct((T, d), BF16)

    tk = min(ATTN_K_TILE, S)
    nt = S // tm
    dq, dk, dvt = pl.pallas_call(
        _proj_diff_kernel, grid=grid,
        in_specs=[x_spec, _const_spec((d, 2 * d)), _const_spec((d, d))] + [_token_spec(tm, LANES)] * 3,
        out_specs=[_token_spec(tm, d), _token_spec(tm, d),
                   pl.BlockSpec((None, DIFF_HEADS, tm // tk, LANES, tk),
                                lambda i: (i // nt, 0, i % nt, 0, 0))],
        out_shape=[act, act, jax.ShapeDtypeStruct((B, DIFF_HEADS, S // tk, LANES, tk), BF16)],
        compiler_params=_params("arbitrary"),
    )(xf, w_diff, w_vt, *rope)

    gq, gk, gv = pl.pallas_call(
        functools.partial(_proj_gdn_kernel, tiles_per_seq=nt), grid=grid,
        in_specs=[x_spec, _const_spec((d, 3 * d)), _const_spec((CONV_WIDTH, 3 * d))],
        out_specs=[_token_spec(tm, d)] * 3, out_shape=[act] * 3,
        scratch_shapes=[pltpu.VMEM((SUBLANES, 3 * d), F32)],
        compiler_params=_params("arbitrary"),
    )(xf, w_gdn, conv_w)

    nba = 2 * GDN_HEADS
    zeros = jnp.zeros((GDN_HEADS,), F32)
    sz, sga, sgb, col = pl.pallas_call(
        _proj_gate_kernel, grid=grid,
        in_specs=[x_spec, _const_spec((d, 3 * d)), _const_spec((d, nba)),
                  _const_spec((1, nba)), _const_spec((1, nba))],
        out_specs=[_token_spec(tm, d)] * 3 + [_token_spec(tm, nba)],
        out_shape=[act] * 3 + [jax.ShapeDtypeStruct((T, nba), F32)],
        compiler_params=_params("arbitrary"),
    )(xf, w_gate, w_ba, jnp.concatenate([zeros, a_log]).reshape(1, nba),
      jnp.concatenate([zeros, dt_bias]).reshape(1, nba))
    gdn_qkv = tuple(t.reshape(B, S, d) for t in (gq, gk, gv))
    return dq, dk, dvt, gdn_qkv, sz, sga, sgb, col


def _diff_attn_kernel(lam_ref, g_ref, q_ref, k_ref, vt_ref, o_ref,
                      sa_ref, sb_ref, m_ref, l_ref, acc_ref, *, lambda_init, tq):
    tk = vt_ref.shape[-1]
    lv = lam_ref[...]
    lam = (jnp.exp(jnp.sum(lv[0:1] * lv[1:2], -1, keepdims=True))
           - jnp.exp(jnp.sum(lv[2:3] * lv[3:4], -1, keepdims=True)) + lambda_init)
    lane = lax.broadcasted_iota(I32, (tq, LANES), 1)
    key = lax.broadcasted_iota(I32, (tk, 2 * tq), 0)
    qry = lax.broadcasted_iota(I32, (tk, 2 * tq), 1)
    key_minus_qry = key - jnp.where(qry >= tq, qry - tq, qry)
    bufs = (sa_ref, sb_ref)

    for qi in range(q_ref.shape[0] // tq):
        q = q_ref[qi * tq:(qi + 1) * tq, :]
        zero = jnp.zeros_like(q)
        q2 = jnp.concatenate([jnp.where(lane < DIFF_HEAD_DIM, q, zero),
                              jnp.where(lane >= DIFF_HEAD_DIM, q, zero)], 0)
        last = (qi * tq) // tk

        def scores(j):
            return _dot_nt(k_ref[j * tk:(j + 1) * tk, :], q2)

        def update(s, j):
            m = m_ref[...]
            m_new = jnp.maximum(m, jnp.max(s, 0, keepdims=True))
            alpha = jnp.exp2(m - m_new)
            p = jnp.exp2(s - m_new)
            l_ref[...] = alpha * l_ref[...] + jnp.sum(p, 0, keepdims=True)
            acc_ref[...] = alpha * acc_ref[...] + _dot(vt_ref[j], p.astype(BF16))
            m_ref[...] = m_new

        m_ref[...] = jnp.full(m_ref.shape, -jnp.inf, F32)
        l_ref[...] = jnp.zeros_like(l_ref)
        acc_ref[...] = jnp.zeros_like(acc_ref)
        bufs[0][...] = scores(0)
        for j in range(last + 1):
            if j < last:
                bufs[(j + 1) % 2][...] = scores(j + 1)
            s = bufs[j % 2][...]
            if j == last:
                s = jnp.where(key_minus_qry <= qi * tq - last * tk, s, -jnp.inf)
            update(s, j)

        o = acc_ref[...] / l_ref[...]
        o = o[:, :tq] - lam * o[:, tq:]
        o = o * lax.rsqrt(jnp.mean(o * o, 0, keepdims=True) + 1e-6) * g_ref[...]
        o_ref[qi * tq:(qi + 1) * tq, :] = (o * (1.0 - lambda_init)).T.astype(BF16)


def _diff_attention(dq, dk, dvt, lam_vecs, subln_g, B, S, lambda_init):
    tq = min(ATTN_Q_TILE, S)
    tk = dvt.shape[-1]
    d = D_MODEL
    q3, k3 = dq.reshape(B, S, d), dk.reshape(B, S, d)
    seq = pl.BlockSpec((None, S, LANES), lambda b, h: (b, 0, h))
    out = pl.pallas_call(
        functools.partial(_diff_attn_kernel, lambda_init=lambda_init, tq=tq),
        grid=(B, DIFF_HEADS),
        in_specs=[_const_spec((4, DIFF_HEAD_DIM)), _const_spec((LANES, 1)), seq, seq,
                  pl.BlockSpec((None, None, S // tk, LANES, tk), lambda b, h: (b, h, 0, 0, 0))],
        out_specs=seq,
        out_shape=jax.ShapeDtypeStruct((B, S, d), BF16),
        scratch_shapes=[pltpu.VMEM((tk, 2 * tq), F32), pltpu.VMEM((tk, 2 * tq), F32),
                        pltpu.VMEM((1, 2 * tq), F32), pltpu.VMEM((1, 2 * tq), F32),
                        pltpu.VMEM((LANES, 2 * tq), F32)],
        compiler_params=_params("arbitrary", "arbitrary"),
    )(lam_vecs, subln_g.reshape(LANES, 1), q3, k3, dvt)
    return out.reshape(B * S, d)


def _gdn_local_kernel(q_ref, k_ref, v_ref, col_ref, row_ref,
                      u_ref, w_ref, qd_ref, kd_ref, qk_ref, gl_ref):
    C = GDN_CHUNK
    R = q_ref.shape[0]
    ri = lax.broadcasted_iota(I32, (R, R), 0)
    ci = lax.broadcasted_iota(I32, (R, R), 1)
    same = (ri // C) == (ci // C)
    incl = same & (ri >= ci)
    strict = same & (ri > ci)
    eye = (ri == ci).astype(F32)
    heads = range(GDN_HEADS)
    head_lanes = [slice(h * GDN_HEAD_DIM, (h + 1) * GDN_HEAD_DIM) for h in heads]
    decays, pbs, t_mats = [], [], []
    for h in heads:
        kh = k_ref[:, head_lanes[h]]
        g_c = col_ref[:, GDN_HEADS + h:GDN_HEADS + h + 1]
        g_r = row_ref[GDN_HEADS + h:GDN_HEADS + h + 1, :]
        decay = jnp.where(incl, jnp.exp(jnp.where(incl, g_c - g_r, 0.0)), 0.0)
        p = -(jnp.where(strict, _dot_nt(kh, kh) * decay, 0.0) * col_ref[:, h:h + 1])
        decays.append(decay)
        pbs.append(p.astype(BF16))
        t_mats.append(eye + p)
    for _ in range(5):
        pbs = [_dot(pb, pb).astype(BF16) for pb in pbs]
        t_mats = [t + _dot(t.astype(BF16), pb) for t, pb in zip(t_mats, pbs)]
    for h in heads:
        lanes = head_lanes[h]
        qh, kh, vh = q_ref[:, lanes], k_ref[:, lanes], v_ref[:, lanes]
        beta_c = col_ref[:, h:h + 1]
        g_c = col_ref[:, GDN_HEADS + h:GDN_HEADS + h + 1]
        decay = decays[h]
        eg_c = jnp.exp(g_c)
        kf = kh.astype(F32)
        rhs = jnp.concatenate([vh.astype(F32) * beta_c, kf * (beta_c * eg_c)], 1)
        uw = _dot(t_mats[h].astype(BF16), rhs.astype(BF16))
        u_ref[:, lanes] = uw[:, :GDN_HEAD_DIM].astype(BF16)
        w_ref[:, lanes] = uw[:, GDN_HEAD_DIM:].astype(BF16)
        qd_ref[:, lanes] = (qh.astype(F32) * eg_c).astype(BF16)
        for c in range(R // C):
            rows = slice(c * C, (c + 1) * C)
            g_last = g_c[(c + 1) * C - 1:(c + 1) * C, :]
            kd_ref[rows, lanes] = (kf[rows] * jnp.exp(g_last - g_c[rows])).astype(BF16)
            gl_ref[c, h:h + 1, :] = jnp.broadcast_to(jnp.exp(g_last), (1, GDN_HEAD_DIM))
            qk_ref[c, h] = (_dot_nt(qh[rows], kh[rows]) * decay[rows, rows]).astype(BF16)


def _gdn_scan_kernel(u_ref, w_ref, qd_ref, kd_ref, qk_ref, gl_ref, sz_ref, ng_ref, o_ref, state_ref):
    @pl.when(pl.program_id(1) == 0)
    def _():
        state_ref[...] = jnp.zeros_like(state_ref)

    C = GDN_CHUNK
    ng = ng_ref[...]
    heads = range(GDN_HEADS)
    lanes = [slice(h * GDN_HEAD_DIM, (h + 1) * GDN_HEAD_DIM) for h in heads]
    for c in range(u_ref.shape[0] // C):
        rows = slice(c * C, (c + 1) * C)
        st = [state_ref[h] for h in heads]
        ws = [_dot(jnp.concatenate([w_ref[rows, lanes[h]], qd_ref[rows, lanes[h]]], 0),
                   st[h].astype(BF16)) for h in heads]
        vnb = [(u_ref[rows, lanes[h]].astype(F32) - ws[h][:C]).astype(BF16) for h in heads]
        for h in heads:
            state_ref[h] = st[h] * gl_ref[c, h:h + 1, :] + _dot_tn(kd_ref[rows, lanes[h]], vnb[h])
        for h in heads:
            o = ws[h][C:] + _dot(qk_ref[c, h], vnb[h])
            o = o * lax.rsqrt(jnp.mean(o * o, -1, keepdims=True) + 1e-6) * ng
            o_ref[rows, lanes[h]] = (o * sz_ref[rows, lanes[h]].astype(F32)).astype(BF16)


def _gated_deltanet(gq, gk, gv, sz, col, norm_g, B, S):
    d = D_MODEL
    C = GDN_CHUNK
    R = min(GDN_GROUP * C, S)
    nc = R // C
    col_arr = col.reshape(B, S, 2 * GDN_HEADS)
    row_arr = col_arr.transpose(0, 2, 1)
    blk = pl.BlockSpec((None, R, d), lambda b, i: (b, i, 0))
    qk_blk = pl.BlockSpec((None, nc, GDN_HEADS, C, C), lambda b, i: (b, i, 0, 0, 0))
    gl_blk = pl.BlockSpec((None, nc, GDN_HEADS, GDN_HEAD_DIM), lambda b, i: (b, i, 0, 0))
    act = jax.ShapeDtypeStruct((B, S, d), BF16)
    u, w, qd, kd, qk, gl = pl.pallas_call(
        _gdn_local_kernel,
        grid=(B, S // R),
        in_specs=[blk, blk, blk,
                  pl.BlockSpec((None, R, 2 * GDN_HEADS), lambda b, i: (b, i, 0)),
                  pl.BlockSpec((None, 2 * GDN_HEADS, R), lambda b, i: (b, 0, i))],
        out_specs=[blk, blk, blk, blk, qk_blk, gl_blk],
        out_shape=[act, act, act, act,
                   jax.ShapeDtypeStruct((B, S // C, GDN_HEADS, C, C), BF16),
                   jax.ShapeDtypeStruct((B, S // C, GDN_HEADS, GDN_HEAD_DIM), F32)],
        compiler_params=_params("arbitrary", "arbitrary"),
    )(gq, gk, gv, col_arr, row_arr)
    out = pl.pallas_call(
        _gdn_scan_kernel,
        grid=(B, S // R),
        in_specs=[blk, blk, blk, blk, qk_blk, gl_blk, blk, _const_spec((1, GDN_HEAD_DIM))],
        out_specs=blk,
        out_shape=act,
        scratch_shapes=[pltpu.VMEM((GDN_HEADS, GDN_HEAD_DIM, GDN_HEAD_DIM), F32)],
        compiler_params=_params("arbitrary", "arbitrary"),
    )(u, w, qd, kd, qk, gl, sz.reshape(B, S, d), norm_g.reshape(1, GDN_HEAD_DIM))
    return out.reshape(B * S, d)


def _mix_kernel(x_ref, od_ref, og_ref, ga_ref, gb_ref, wd_ref, wg_ref, wm_ref, g_ref, b_ref,
                o_ref, *, alpha):
    yd = _dot(od_ref[...], wd_ref[...])
    yg = _dot(og_ref[...], wg_ref[...])
    m = ga_ref[...].astype(F32) * yd + gb_ref[...].astype(F32) * yg
    r = alpha * x_ref[...] + _dot(m.astype(BF16), wm_ref[...])
    o_ref[...] = _layer_norm(r, g_ref[...], b_ref[...])


def _mix(xf, od, og, sga, sgb, w_diff_o, w_gdn_o, w_mix_o, ln_g, ln_b, alpha):
    T, d = xf.shape
    tm = min(TOKEN_TILE, T)
    tok = _token_spec(tm, d)
    wspec = _const_spec((d, d))
    vec = _const_spec((1, d))
    return pl.pallas_call(
        functools.partial(_mix_kernel, alpha=alpha), grid=(T // tm,),
        in_specs=[tok] * 5 + [wspec] * 3 + [vec] * 2,
        out_specs=tok, out_shape=jax.ShapeDtypeStruct((T, d), F32),
        compiler_params=_params("arbitrary"),
    )(xf, od, og, sga, sgb, w_diff_o.astype(BF16), w_gdn_o.astype(BF16), w_mix_o.astype(BF16),
      ln_g.reshape(1, d), ln_b.reshape(1, d))


def _cross_kernel(x_ref, k_ref, v_ref, wq_ref, wo_ref, g_ref, b_ref, wr_ref, br_ref,
                  o_ref, idx_ref, gate_ref, *, alpha):
    half = x_ref.shape[0] // 2
    for part in range(2):
        rows = slice(part * half, (part + 1) * half)
        x = x_ref[rows, :]
        q = (_dot(x.astype(BF16), wq_ref[...]) * (MEM_HEAD_DIM ** -0.5)).astype(BF16)
        outs = []
        for h in range(MEM_HEADS):
            lanes = slice(h * MEM_HEAD_DIM, (h + 1) * MEM_HEAD_DIM)
            s = _dot_nt(q[:, lanes], k_ref[:, lanes])
            p = jnp.exp(s - jnp.max(s, -1, keepdims=True))
            outs.append(_dot(p.astype(BF16), v_ref[:, lanes]) / jnp.sum(p, -1, keepdims=True))
        o = jnp.concatenate(outs, 1).astype(BF16)
        r = alpha * x + _dot(o, wo_ref[...])
        x2 = _layer_norm(r, g_ref[...], b_ref[...])
        o_ref[rows, :] = x2

        logits = _dot(x2.astype(BF16), wr_ref[...]) + br_ref[...]
        lane = lax.broadcasted_iota(I32, logits.shape, 1)
        vals, idxs = [], []
        for _ in range(TOP_K):
            mx = jnp.max(logits, -1, keepdims=True)
            ix = jnp.min(jnp.where(logits == mx, lane, N_EXPERTS), -1, keepdims=True)
            vals.append(mx)
            idxs.append(ix)
            logits = jnp.where(lane == ix, -jnp.inf, logits)
        e = jnp.exp(jnp.concatenate(vals, 1) - vals[0])
        gate_ref[rows, :] = e / jnp.sum(e, -1, keepdims=True)
        idx_ref[rows, :] = jnp.concatenate(idxs, 1)


def _cross_attention(x1, mem, w_cq, w_ck, w_cv, w_co, ln_g, ln_b, w_router, b_router, B, S, alpha):
    T, d = x1.shape
    M = mem.shape[1]
    w_kv = jnp.concatenate([w_ck, w_cv], 1).astype(BF16)
    tmem = min(TOKEN_TILE, B * M)
    kv = pl.pallas_call(
        _proj_plain_kernel, grid=(B * M // tmem,),
        in_specs=[_token_spec(tmem, d), _const_spec((d, 2 * d))],
        out_specs=_token_spec(tmem, 2 * d),
        out_shape=jax.ShapeDtypeStruct((B * M, 2 * d), BF16),
        compiler_params=_params("arbitrary"),
    )(mem.reshape(B * M, d), w_kv).reshape(B, M, 2 * d)

    tm = min(TOKEN_TILE, S)
    nt = S // tm
    tok = pl.BlockSpec((tm, d), lambda b, i: (b * nt + i, 0))
    small = pl.BlockSpec((tm, TOP_K), lambda b, i: (b * nt + i, 0))
    return pl.pallas_call(
        functools.partial(_cross_kernel, alpha=alpha), grid=(B, nt),
        in_specs=[tok,
                  pl.BlockSpec((None, M, d), lambda b, i: (b, 0, 0)),
                  pl.BlockSpec((None, M, d), lambda b, i: (b, 0, 1)),
                  _const_spec((d, d)), _const_spec((d, d)), _const_spec((1, d)), _const_spec((1, d)),
                  _const_spec((d, N_EXPERTS)), _const_spec((1, N_EXPERTS))],
        out_specs=[tok, small, small],
        out_shape=[jax.ShapeDtypeStruct((T, d), F32), jax.ShapeDtypeStruct((T, TOP_K), I32),
                   jax.ShapeDtypeStruct((T, TOP_K), F32)],
        compiler_params=_params("arbitrary", "arbitrary"),
    )(x1, kv, kv, w_cq.astype(BF16), w_co.astype(BF16), ln_g.reshape(1, d), ln_b.reshape(1, d),
      w_router.astype(BF16), b_router.reshape(1, N_EXPERTS))


def _rank_kernel(idx_ref, rank_ref, cnt_ref, run_ref):
    @pl.when(pl.program_id(0) == 0)
    def _():
        run_ref[...] = jnp.zeros_like(run_ref)

    idx = idx_ref[...]
    tm = idx.shape[0]
    lane = lax.broadcasted_iota(I32, (tm, N_EXPERTS), 1)
    hits = [lane == idx[:, k:k + 1] for k in range(TOP_K)]
    onehot = sum(hit.astype(F32) for hit in hits)
    r = lax.broadcasted_iota(I32, (tm, tm), 0)
    c = lax.broadcasted_iota(I32, (tm, tm), 1)
    lower = (r > c).astype(BF16)
    before = _dot(lower, onehot.astype(BF16)) + run_ref[...]
    ranks = [jnp.sum(jnp.where(hit, before, 0.0), -1, keepdims=True) for hit in hits]
    rank_ref[...] = jnp.concatenate(ranks, 1).astype(I32)
    run_ref[...] = run_ref[...] + jnp.sum(onehot, 0, keepdims=True)
    cnt_ref[...] = run_ref[...]


def _dest_kernel(idx_ref, rank_ref, start_ref, dest_ref):
    idx = idx_ref[...]
    lane = lax.broadcasted_iota(I32, (idx.shape[0], N_EXPERTS), 1)
    starts = start_ref[...]
    cols = [jnp.sum(jnp.where(lane == idx[:, k:k + 1], starts, 0), -1, keepdims=True)
            for k in range(TOP_K)]
    dest_ref[...] = rank_ref[...] + jnp.concatenate(cols, 1)


def _row_copy(src_ref, src_row, dst_ref, dst_row, sem):
    return pltpu.make_async_copy(src_ref.at[pl.ds(src_row, 1), :], dst_ref.at[pl.ds(dst_row, 1), :], sem)


def _index_copy(dest_hbm, dest_smem, sem_idx, step, slot, n):
    return pltpu.make_async_copy(dest_hbm.at[pl.ds(pl.multiple_of(step * n, n), n)],
                                 dest_smem.at[pl.ds(pl.multiple_of(slot * n, n), n)], sem_idx.at[slot])


def _dispatch_kernel(zero_rows, dest_hbm, x_ref, xs_hbm, dest_smem, zero_buf, sem_idx, sem_rows):
    tm = x_ref.shape[0]
    n = tm * TOP_K
    i = pl.program_id(0)
    last = pl.num_programs(0) - 1
    slot = i % 2

    @pl.when(i == 0)
    def _():
        _index_copy(dest_hbm, dest_smem, sem_idx, 0, 0, n).start()
        zero_buf[...] = jnp.zeros_like(zero_buf)
        bm = zero_buf.shape[0]

        def zero_tile(t):
            row = pl.multiple_of(jnp.maximum(zero_rows[t], 0), bm)
            return pltpu.make_async_copy(zero_buf, xs_hbm.at[pl.ds(row, bm), :], sem_rows)

        for t in range(zero_rows.shape[0]):
            pl.when(zero_rows[t] >= 0)(lambda t=t: zero_tile(t).start())
        for t in range(zero_rows.shape[0]):
            pl.when(zero_rows[t] >= 0)(lambda t=t: zero_tile(t).wait())

    _index_copy(dest_hbm, dest_smem, sem_idx, i, slot, n).wait()

    @pl.when(i < last)
    def _():
        _index_copy(dest_hbm, dest_smem, sem_idx, i + 1, 1 - slot, n).start()

    base = slot * n

    def issue(jj, _):
        for u in range(ROW_UNROLL):
            j = jj * ROW_UNROLL + u
            tok = jj * (ROW_UNROLL // TOP_K) + u // TOP_K
            _row_copy(x_ref, tok, xs_hbm, dest_smem[base + j], sem_rows).start(priority=u % 2)
        return 0

    lax.fori_loop(0, n // ROW_UNROLL, issue, 0)
    for _ in range(TOP_K):
        pltpu.make_async_copy(x_ref, xs_hbm.at[pl.ds(0, tm), :], sem_rows).wait()


def _expert_kernel(te_ref, tv_ref, xs_ref, w1_ref, b1_ref, w2_ref, b2_ref, ys_ref):
    i = pl.program_id(0)

    @pl.when(tv_ref[i] != 0)
    def _():
        ff = w2_ref.shape[0]
        h = _dot(xs_ref[...].astype(BF16), w1_ref[...]) + b1_ref[...]
        h_gate = jnp.minimum(h[:, :ff], SWIGLU_LIMIT)
        h_up = jnp.clip(h[:, ff:], -SWIGLU_LIMIT, SWIGLU_LIMIT)
        act = h_gate * _sigmoid(SWIGLU_ALPHA * h_gate) * (h_up + 1.0)
        ys_ref[...] = _dot(act.astype(BF16), w2_ref[...]) + b2_ref[...]

    @pl.when(tv_ref[i] == 0)
    def _():
        ys_ref[...] = jnp.zeros_like(ys_ref)


def _combine_kernel(dest_hbm, ys_hbm, x_ref, gate_ref, g_ref, b_ref, o_ref,
                    dest_smem, buf, sem_idx, sem_rows, *, alpha):
    tm = x_ref.shape[0]
    n = tm * TOP_K
    i = pl.program_id(0)
    last = pl.num_programs(0) - 1
    slot = i % 2

    def gather_tile(step, s):
        cp = _index_copy(dest_hbm, dest_smem, sem_idx, step, s, n)
        cp.start()
        cp.wait()
        base = s * n

        def issue(jj, _):
            for u in range(ROW_UNROLL):
                j = jj * ROW_UNROLL + u
                tok = jj * (ROW_UNROLL // TOP_K) + u // TOP_K
                _row_copy(ys_hbm, dest_smem[base + j], buf.at[s, u % TOP_K], tok,
                          sem_rows.at[s]).start(priority=u % 2)
            return 0

        lax.fori_loop(0, n // ROW_UNROLL, issue, 0)

    @pl.when(i == 0)
    def _():
        gather_tile(0, 0)

    @pl.when(i < last)
    def _():
        gather_tile(i + 1, 1 - slot)

    for k in range(TOP_K):
        pltpu.make_async_copy(ys_hbm.at[pl.ds(0, tm), :], buf.at[slot, k], sem_rows.at[slot]).wait()
    gates = gate_ref[...]
    y = sum(gates[:, k:k + 1] * buf[slot, k] for k in range(TOP_K))
    o_ref[...] = _layer_norm(alpha * x_ref[...] + y, g_ref[...], b_ref[...])


def _moe(x2, idx, gates, w1, b1, w2, b2, ln_g, ln_b, alpha):
    T, d = x2.shape
    ff = w2.shape[1]
    tk = T * TOP_K
    bm = EXPERT_TILE
    n_tiles = -(-tk // bm) + N_EXPERTS
    R = n_tiles * bm

    tr = min(TOKEN_TILE, T)
    small = lambda tm: pl.BlockSpec((tm, TOP_K), lambda i: (i, 0))
    rank, counts = pl.pallas_call(
        _rank_kernel, grid=(T // tr,),
        in_specs=[small(tr)], out_specs=[small(tr), _const_spec((1, N_EXPERTS))],
        out_shape=[jax.ShapeDtypeStruct((T, TOP_K), I32), jax.ShapeDtypeStruct((1, N_EXPERTS), F32)],
        scratch_shapes=[pltpu.VMEM((1, N_EXPERTS), F32)],
        compiler_params=_params("arbitrary"),
    )(idx)

    counts = counts.reshape(N_EXPERTS).astype(I32)
    padded = (counts + bm - 1) // bm * bm
    pad_ends = jnp.cumsum(padded)
    pad_starts = pad_ends - padded
    tile_row = jnp.arange(n_tiles, dtype=I32) * bm
    tile_e = jnp.minimum(jnp.sum(pad_ends[None, :] <= tile_row[:, None], 1), N_EXPERTS - 1).astype(I32)
    tile_valid = (tile_row < pad_ends[-1]).astype(I32)

    dest = pl.pallas_call(
        _dest_kernel, grid=(T // tr,),
        in_specs=[small(tr), small(tr), _const_spec((1, N_EXPERTS))], out_specs=small(tr),
        out_shape=jax.ShapeDtypeStruct((T, TOP_K), I32),
        compiler_params=_params("arbitrary"),
    )(idx, rank, pad_starts.reshape(1, N_EXPERTS)).reshape(tk)

    tm = min(ROUTE_TILE, T)
    n = tm * TOP_K
    any_spec = pl.BlockSpec(memory_space=pl.ANY)
    last_tile = jnp.where(padded > 0, pad_ends - bm, -1)
    tail = pad_ends[-1] + jnp.arange(n_tiles - tk // bm, dtype=I32) * bm
    tail = jnp.where(tail < R, tail, -1)
    zero_rows = jnp.concatenate([last_tile, tail]).astype(I32)
    xs = pl.pallas_call(
        _dispatch_kernel,
        grid_spec=pltpu.PrefetchScalarGridSpec(
            num_scalar_prefetch=1, grid=(T // tm,),
            in_specs=[any_spec, pl.BlockSpec((tm, d), lambda i, zr: (i, 0))], out_specs=any_spec,
            scratch_shapes=[pltpu.SMEM((2 * n,), I32), pltpu.VMEM((bm, d), F32),
                            pltpu.SemaphoreType.DMA((2,)), pltpu.SemaphoreType.DMA(())]),
        out_shape=jax.ShapeDtypeStruct((R, d), F32),
        compiler_params=_params("arbitrary"),
    )(zero_rows, dest, x2)

    ys = pl.pallas_call(
        _expert_kernel,
        grid_spec=pltpu.PrefetchScalarGridSpec(
            num_scalar_prefetch=2, grid=(n_tiles,),
            in_specs=[pl.BlockSpec((bm, d), lambda i, te, tv: (i, 0)),
                      pl.BlockSpec((None, d, 2 * ff), lambda i, te, tv: (te[i], 0, 0)),
                      pl.BlockSpec((None, 1, 2 * ff), lambda i, te, tv: (te[i], 0, 0)),
                      pl.BlockSpec((None, ff, d), lambda i, te, tv: (te[i], 0, 0)),
                      pl.BlockSpec((None, 1, d), lambda i, te, tv: (te[i], 0, 0))],
            out_specs=pl.BlockSpec((bm, d), lambda i, te, tv: (i, 0))),
        out_shape=jax.ShapeDtypeStruct((R, d), F32),
        compiler_params=_params("arbitrary"),
    )(tile_e, tile_valid, xs, w1.astype(BF16), b1.reshape(N_EXPERTS, 1, 2 * ff),
      w2.astype(BF16), b2.reshape(N_EXPERTS, 1, d))

    return pl.pallas_call(
        functools.partial(_combine_kernel, alpha=alpha), grid=(T // tm,),
        in_specs=[any_spec, any_spec, _token_spec(tm, d), small(tm), _const_spec((1, d)), _const_spec((1, d))],
        out_specs=_token_spec(tm, d),
        out_shape=jax.ShapeDtypeStruct((T, d), F32),
        scratch_shapes=[pltpu.SMEM((2 * n,), I32), pltpu.VMEM((2, TOP_K, tm, d), F32),
                        pltpu.SemaphoreType.DMA((2,)), pltpu.SemaphoreType.DMA((2,))],
        compiler_params=_params("arbitrary"),
    )(dest, ys, x2, gates, ln_g.reshape(1, d), ln_b.reshape(1, d))


def kernel(x, mem, positions, w_in, diff_lambda_q1, diff_lambda_k1, diff_lambda_q2, diff_lambda_k2, diff_subln_g, w_diff_o, gdn_conv_w, gdn_A_log, gdn_dt_bias, gdn_norm_g, w_gdn_o, w_mix_o, ln1_g, ln1_b, w_cq, w_ck, w_cv, w_co, ln2_g, ln2_b, w_router, b_router, w_exp_in, b_exp_in, w_exp_out, b_exp_out, ln3_g, ln3_b):
    B, S, d = x.shape
    depth = w_in.shape[0]
    alpha = (2 * depth) ** 0.25
    rope = _rope_tables(positions)
    xf = x.reshape(B * S, d)
    for l in range(depth):
        lambda_init = 0.8 - 0.6 * math.exp(-0.3 * l)
        dq, dk, dvt, (gq, gk, gv), sz, sga, sgb, gdn_col = _input_projections(
            xf, w_in[l], rope, gdn_conv_w[l], gdn_A_log[l], gdn_dt_bias[l], B, S)
        lam_vecs = jnp.stack([diff_lambda_q1[l], diff_lambda_k1[l], diff_lambda_q2[l], diff_lambda_k2[l]])
        od = _diff_attention(dq, dk, dvt, lam_vecs, diff_subln_g[l], B, S, lambda_init)
        og = _gated_deltanet(gq, gk, gv, sz, gdn_col, gdn_norm_g[l], B, S)
        x1 = _mix(xf, od, og, sga, sgb, w_diff_o[l], w_gdn_o[l], w_mix_o[l], ln1_g[l], ln1_b[l], alpha)
        x2, idx, gates = _cross_attention(x1, mem, w_cq[l], w_ck[l], w_cv[l], w_co[l], ln2_g[l], ln2_b[l],
                                          w_router[l], b_router[l], B, S, alpha)
        xf = _moe(x2, idx, gates, w_exp_in[l], b_exp_in[l], w_exp_out[l], b_exp_out[l],
                  ln3_g[l], ln3_b[l], alpha)
    return xf.reshape(B, S, d)
```

```python
import functools
import math

import jax
import jax.numpy as jnp
from jax import lax
from jax.experimental import pallas as pl
from jax.experimental.pallas import tpu as pltpu

F32 = jnp.float32
BF16 = jnp.bfloat16
I32 = jnp.int32

D_MODEL = 1024
DIFF_HEADS = 8
DIFF_HEAD_DIM = 64
ROPE_THETA = 500000.0
ROT_DIM = DIFF_HEAD_DIM // 4
ROT_HALF = ROT_DIM // 2
GDN_HEADS = 8
GDN_HEAD_DIM = 128
CONV_WIDTH = 4
GDN_CHUNK = 64
MEM_HEADS = 4
MEM_HEAD_DIM = D_MODEL // MEM_HEADS
N_EXPERTS = 32
TOP_K = 4
SWIGLU_LIMIT = 7.0
SWIGLU_ALPHA = 1.702
LANES = 128
SUBLANES = 8
VMEM_LIMIT = 48 * 1024 * 1024

TOKEN_TILE = 1024
ATTN_Q_TILE = 512
ATTN_K_TILE = 512
GDN_GROUP = 4
ROUTE_TILE = 512
EXPERT_TILE = 512
ROW_UNROLL = 32


def _params(*sem):
    return pltpu.CompilerParams(dimension_semantics=sem, vmem_limit_bytes=VMEM_LIMIT)


def _dot(a, b):
    return jnp.dot(a, b, preferred_element_type=F32)


def _dot_nt(a, b):
    return lax.dot_general(a, b, (((1,), (1,)), ((), ())), preferred_element_type=F32)


def _dot_tn(a, b):
    return lax.dot_general(a, b, (((0,), (0,)), ((), ())), preferred_element_type=F32)


def _sigmoid(x):
    return 1.0 / (1.0 + jnp.exp(-x))


def _layer_norm(r, g, b):
    mu = jnp.mean(r, -1, keepdims=True)
    d = r - mu
    var = jnp.mean(d * d, -1, keepdims=True)
    return d * lax.rsqrt(var + 1e-5) * g + b


def _rope_kernel(pos_ref, inv_ref, cos_ref, sa_ref, sb_ref):
    ang = pos_ref[...] * inv_ref[...]
    sin = jnp.sin(ang)
    first_half = lax.broadcasted_iota(I32, ang.shape, 1) % DIFF_HEAD_DIM < ROT_HALF
    cos_ref[...] = jnp.cos(ang)
    sa_ref[...] = jnp.where(first_half, -sin, 0.0)
    sb_ref[...] = jnp.where(first_half, 0.0, sin)


def _rope_tables(positions):
    T = positions.size
    inv_freq = jnp.power(ROPE_THETA, -jnp.arange(0, ROT_DIM, 2, dtype=F32) / ROT_DIM)
    inv = jnp.concatenate([inv_freq, inv_freq, jnp.zeros((DIFF_HEAD_DIM - ROT_DIM,), F32)])
    inv = jnp.tile(inv, LANES // DIFF_HEAD_DIM).reshape(1, LANES)
    pos = jnp.broadcast_to(positions.reshape(T, 1).astype(F32), (T, LANES))
    rb = min(T, 1024)
    spec = pl.BlockSpec((rb, LANES), lambda i: (i, 0))
    return pl.pallas_call(
        _rope_kernel,
        grid=(T // rb,),
        in_specs=[spec, _const_spec((1, LANES))],
        out_specs=[spec] * 3,
        out_shape=[jax.ShapeDtypeStruct((T, LANES), F32)] * 3,
        compiler_params=_params("arbitrary"),
    )(pos, inv)


def _proj_diff_kernel(x_ref, w_ref, wvt_ref, cos_ref, sa_ref, sb_ref, q_ref, k_ref, vt_ref):
    xb = x_ref[...].astype(BF16)
    cos_t, sin_a, sin_b = cos_ref[...], sa_ref[...], sb_ref[...]
    qk_w = DIFF_HEADS * 2 * DIFF_HEAD_DIM
    for part, o_ref in enumerate((q_ref, k_ref)):
        acc = _dot(xb, w_ref[:, part * qk_w:(part + 1) * qk_w])
        for h in range(DIFF_HEADS):
            t = acc[:, h * LANES:(h + 1) * LANES]
            r = (t * cos_t + pltpu.roll(t, LANES - ROT_HALF, 1) * sin_a
                 + pltpu.roll(t, ROT_HALF, 1) * sin_b)
            if part == 0:
                r = r * (DIFF_HEAD_DIM ** -0.5 * math.log2(math.e))
            o_ref[:, h * LANES:(h + 1) * LANES] = r.astype(BF16)
    vt = _dot_nt(wvt_ref[...], xb)
    tk = vt_ref.shape[-1]
    for h in range(DIFF_HEADS):
        for c in range(vt_ref.shape[1]):
            vt_ref[h, c] = vt[h * LANES:(h + 1) * LANES, c * tk:(c + 1) * tk].astype(BF16)


def _proj_plain_kernel(x_ref, w_ref, o_ref):
    o_ref[...] = _dot(x_ref[...].astype(BF16), w_ref[...]).astype(BF16)


def _proj_gdn_kernel(x_ref, w_ref, cw_ref, q_ref, k_ref, v_ref, tail_ref, *, tiles_per_seq):
    xb = x_ref[...].astype(BF16)
    tm = xb.shape[0]
    d = q_ref.shape[1]
    first = pl.program_id(0) % tiles_per_seq == 0
    row8 = lax.broadcasted_iota(I32, (SUBLANES, d), 0)
    for part, o_ref in enumerate((q_ref, k_ref, v_ref)):
        cols = slice(part * d, (part + 1) * d)
        acc = _dot(xb, w_ref[:, cols])
        prev = jnp.where(first, 0.0, tail_ref[:, cols])
        tail_ref[:, cols] = acc[tm - SUBLANES:, :]
        y = acc * cw_ref[CONV_WIDTH - 1:CONV_WIDTH, cols]
        head = y[:SUBLANES]
        for j in range(1, CONV_WIDTH):
            cw_j = cw_ref[CONV_WIDTH - 1 - j:CONV_WIDTH - j, cols]
            shifted = pltpu.roll(acc, j, 0)
            y = y + shifted * cw_j
            head = head + jnp.where(row8 >= j, shifted[:SUBLANES], pltpu.roll(prev, j, 0)) * cw_j
        y = jnp.concatenate([head, y[SUBLANES:]], 0)
        y = y * _sigmoid(y)
        if part == 2:
            o_ref[...] = y.astype(BF16)
            continue
        for h in range(d // GDN_HEAD_DIM):
            lanes = slice(h * GDN_HEAD_DIM, (h + 1) * GDN_HEAD_DIM)
            yh = y[:, lanes]
            yh = yh * lax.rsqrt(jnp.sum(yh * yh, -1, keepdims=True) + 1e-6)
            if part == 0:
                yh = yh * (GDN_HEAD_DIM ** -0.5)
            o_ref[:, lanes] = yh.astype(BF16)


def _proj_gate_kernel(x_ref, w_ref, wba_ref, alog_ref, dtb_ref, sz_ref, ga_ref, gb_ref, col_ref):
    xb = x_ref[...].astype(BF16)
    z = _dot(xb, w_ref[:, :D_MODEL])
    sz_ref[...] = (z * _sigmoid(z)).astype(BF16)
    ga_ref[...] = _sigmoid(_dot(xb, w_ref[:, D_MODEL:2 * D_MODEL])).astype(BF16)
    gb_ref[...] = _sigmoid(_dot(xb, w_ref[:, 2 * D_MODEL:])).astype(BF16)

    ba = _dot(xb, wba_ref[...])
    lane = lax.broadcasted_iota(I32, ba.shape, 1)
    row = lax.broadcasted_iota(I32, ba.shape, 0)
    s = ba + dtb_ref[...]
    softplus = jnp.maximum(s, 0.0) + jnp.log1p(jnp.exp(-jnp.abs(s)))
    g = -jnp.exp(alog_ref[...]) * softplus
    shift = 1
    while shift < GDN_CHUNK:
        g = g + jnp.where(row % GDN_CHUNK >= shift, pltpu.roll(g, shift, 0), 0.0)
        shift *= 2
    col_ref[...] = jnp.where(lane < GDN_HEADS, _sigmoid(ba), g)


def _token_spec(tm, width):
    return pl.BlockSpec((tm, width), lambda i: (i, 0))


def _const_spec(shape):
    return pl.BlockSpec(shape, lambda *_: (0,) * len(shape))


def _input_projections(xf, w_in, rope, conv_w, a_log, dt_bias, B, S):
    T = xf.shape[0]
    tm = min(TOKEN_TILE, S)
    grid = (T // tm,)
    d = D_MODEL
    w_diff = w_in[:, :2 * d].astype(BF16)
    w_vt = w_in[:, 2 * d:3 * d].T.astype(BF16)
    w_gdn = w_in[:, 3 * d:6 * d].astype(BF16)
    w_z = w_in[:, 6 * d:7 * d]
    w_ba = w_in[:, 7 * d:7 * d + 2 * GDN_HEADS].astype(BF16)
    w_gate = jnp.concatenate([w_z, w_in[:, 7 * d + 2 * GDN_HEADS:]], 1).astype(BF16)
    x_spec = _token_spec(tm, d)
    act = jax.ShapeDtypeStruct((T, d), BF16)

    tk = min(ATTN_K_TILE, S)
    nt = S // tm
    dq, dk, dvt = pl.pallas_call(
        _proj_diff_kernel, grid=grid,
        in_specs=[x_spec, _const_spec((d, 2 * d)), _const_spec((d, d))] + [_token_spec(tm, LANES)] * 3,
        out_specs=[_token_spec(tm, d), _token_spec(tm, d),
                   pl.BlockSpec((None, DIFF_HEADS, tm // tk, LANES, tk),
                                lambda i: (i // nt, 0, i % nt, 0, 0))],
        out_shape=[act, act, jax.ShapeDtypeStruct((B, DIFF_HEADS, S // tk, LANES, tk), BF16)],
        compiler_params=_params("arbitrary"),
    )(xf, w_diff, w_vt, *rope)

    gq, gk, gv = pl.pallas_call(
        functools.partial(_proj_gdn_kernel, tiles_per_seq=nt), grid=grid,
        in_specs=[x_spec, _const_spec((d, 3 * d)), _const_spec((CONV_WIDTH, 3 * d))],
        out_specs=[_token_spec(tm, d)] * 3, out_shape=[act] * 3,
        scratch_shapes=[pltpu.VMEM((SUBLANES, 3 * d), F32)],
        compiler_params=_params("arbitrary"),
    )(xf, w_gdn, conv_w)

    nba = 2 * GDN_HEADS
    zeros = jnp.zeros((GDN_HEADS,), F32)
    sz, sga, sgb, col = pl.pallas_call(
        _proj_gate_kernel, grid=grid,
        in_specs=[x_spec, _const_spec((d, 3 * d)), _const_spec((d, nba)),
                  _const_spec((1, nba)), _const_spec((1, nba))],
        out_specs=[_token_spec(tm, d)] * 3 + [_token_spec(tm, nba)],
        out_shape=[act] * 3 + [jax.ShapeDtypeStruct((T, nba), F32)],
        compiler_params=_params("arbitrary"),
    )(xf, w_gate, w_ba, jnp.concatenate([zeros, a_log]).reshape(1, nba),
      jnp.concatenate([zeros, dt_bias]).reshape(1, nba))
    gdn_qkv = tuple(t.reshape(B, S, d) for t in (gq, gk, gv))
    return dq, dk, dvt, gdn_qkv, sz, sga, sgb, col


def _diff_attn_kernel(lam_ref, g_ref, q_ref, k_ref, vt_ref, o_ref,
                      sa_ref, sb_ref, m_ref, l_ref, acc_ref, *, lambda_init, tq):
    tk = vt_ref.shape[-1]
    lv = lam_ref[...]
    lam = (jnp.exp(jnp.sum(lv[0:1] * lv[1:2], -1, keepdims=True))
           - jnp.exp(jnp.sum(lv[2:3] * lv[3:4], -1, keepdims=True)) + lambda_init)
    lane = lax.broadcasted_iota(I32, (tq, LANES), 1)
    key = lax.broadcasted_iota(I32, (tk, 2 * tq), 0)
    qry = lax.broadcasted_iota(I32, (tk, 2 * tq), 1)
    key_minus_qry = key - jnp.where(qry >= tq, qry - tq, qry)
    bufs = (sa_ref, sb_ref)

    for qi in range(q_ref.shape[0] // tq):
        q = q_ref[qi * tq:(qi + 1) * tq, :]
        zero = jnp.zeros_like(q)
        q2 = jnp.concatenate([jnp.where(lane < DIFF_HEAD_DIM, q, zero),
                              jnp.where(lane >= DIFF_HEAD_DIM, q, zero)], 0)
        last = (qi * tq) // tk

        def scores(j):
            return _dot_nt(k_ref[j * tk:(j + 1) * tk, :], q2)

        def update(s, j):
            m = m_ref[...]
            m_new = jnp.maximum(m, jnp.max(s, 0, keepdims=True))
            alpha = jnp.exp2(m - m_new)
            p = jnp.exp2(s - m_new)
            l_ref[...] = alpha * l_ref[...] + jnp.sum(p, 0, keepdims=True)
            acc_ref[...] = alpha * acc_ref[...] + _dot(vt_ref[j], p.astype(BF16))
            m_ref[...] = m_new

        m_ref[...] = jnp.full(m_ref.shape, -jnp.inf, F32)
        l_ref[...] = jnp.zeros_like(l_ref)
        acc_ref[...] = jnp.zeros_like(acc_ref)
        bufs[0][...] = scores(0)
        for j in range(last + 1):
            if j < last:
                bufs[(j + 1) % 2][...] = scores(j + 1)
            s = bufs[j % 2][...]
            if j == last:
                s = jnp.where(key_minus_qry <= qi * tq - last * tk, s, -jnp.inf)
            update(s, j)

        o = acc_ref[...] / l_ref[...]
        o = o[:, :tq] - lam * o[:, tq:]
        o = o * lax.rsqrt(jnp.mean(o * o, 0, keepdims=True) + 1e-6) * g_ref[...]
        o_ref[qi * tq:(qi + 1) * tq, :] = (o * (1.0 - lambda_init)).T.astype(BF16)


def _diff_attention(dq, dk, dvt, lam_vecs, subln_g, B, S, lambda_init):
    tq = min(ATTN_Q_TILE, S)
    tk = dvt.shape[-1]
    d = D_MODEL
    q3, k3 = dq.reshape(B, S, d), dk.reshape(B, S, d)
    seq = pl.BlockSpec((None, S, LANES), lambda b, h: (b, 0, h))
    out = pl.pallas_call(
        functools.partial(_diff_attn_kernel, lambda_init=lambda_init, tq=tq),
        grid=(B, DIFF_HEADS),
        in_specs=[_const_spec((4, DIFF_HEAD_DIM)), _const_spec((LANES, 1)), seq, seq,
                  pl.BlockSpec((None, None, S // tk, LANES, tk), lambda b, h: (b, h, 0, 0, 0))],
        out_specs=seq,
        out_shape=jax.ShapeDtypeStruct((B, S, d), BF16),
        scratch_shapes=[pltpu.VMEM((tk, 2 * tq), F32), pltpu.VMEM((tk, 2 * tq), F32),
                        pltpu.VMEM((1, 2 * tq), F32), pltpu.VMEM((1, 2 * tq), F32),
                        pltpu.VMEM((LANES, 2 * tq), F32)],
        compiler_params=_params("arbitrary", "arbitrary"),
    )(lam_vecs, subln_g.reshape(LANES, 1), q3, k3, dvt)
    return out.reshape(B * S, d)


def _gdn_local_kernel(q_ref, k_ref, v_ref, col_ref, row_ref,
                      u_ref, w_ref, qd_ref, kd_ref, qk_ref, gl_ref):
    C = GDN_CHUNK
    R = q_ref.shape[0]
    ri = lax.broadcasted_iota(I32, (R, R), 0)
    ci = lax.broadcasted_iota(I32, (R, R), 1)
    same = (ri // C) == (ci // C)
    incl = same & (ri >= ci)
    strict = same & (ri > ci)
    eye = (ri == ci).astype(F32)
    heads = range(GDN_HEADS)
    head_lanes = [slice(h * GDN_HEAD_DIM, (h + 1) * GDN_HEAD_DIM) for h in heads]
    decays, pbs, t_mats = [], [], []
    for h in heads:
        kh = k_ref[:, head_lanes[h]]
        g_c = col_ref[:, GDN_HEADS + h:GDN_HEADS + h + 1]
        g_r = row_ref[GDN_HEADS + h:GDN_HEADS + h + 1, :]
        decay = jnp.where(incl, jnp.exp(jnp.where(incl, g_c - g_r, 0.0)), 0.0)
        p = -(jnp.where(strict, _dot_nt(kh, kh) * decay, 0.0) * col_ref[:, h:h + 1])
        decays.append(decay)
        pbs.append(p.astype(BF16))
        t_mats.append(eye + p)
    for _ in range(5):
        pbs = [_dot(pb, pb).astype(BF16) for pb in pbs]
        t_mats = [t + _dot(t.astype(BF16), pb) for t, pb in zip(t_mats, pbs)]
    for h in heads:
        lanes = head_lanes[h]
        qh, kh, vh = q_ref[:, lanes], k_ref[:, lanes], v_ref[:, lanes]
        beta_c = col_ref[:, h:h + 1]
        g_c = col_ref[:, GDN_HEADS + h:GDN_HEADS + h + 1]
        decay = decays[h]
        eg_c = jnp.exp(g_c)
        kf = kh.astype(F32)
        rhs = jnp.concatenate([vh.astype(F32) * beta_c, kf * (beta_c * eg_c)], 1)
        uw = _dot(t_mats[h].astype(BF16), rhs.astype(BF16))
        u_ref[:, lanes] = uw[:, :GDN_HEAD_DIM].astype(BF16)
        w_ref[:, lanes] = uw[:, GDN_HEAD_DIM:].astype(BF16)
        qd_ref[:, lanes] = (qh.astype(F32) * eg_c).astype(BF16)
        for c in range(R // C):
            rows = slice(c * C, (c + 1) * C)
            g_last = g_c[(c + 1) * C - 1:(c + 1) * C, :]
            kd_ref[rows, lanes] = (kf[rows] * jnp.exp(g_last - g_c[rows])).astype(BF16)
            gl_ref[c, h:h + 1, :] = jnp.broadcast_to(jnp.exp(g_last), (1, GDN_HEAD_DIM))
            qk_ref[c, h] = (_dot_nt(qh[rows], kh[rows]) * decay[rows, rows]).astype(BF16)


def _gdn_scan_kernel(u_ref, w_ref, qd_ref, kd_ref, qk_ref, gl_ref, sz_ref, ng_ref, o_ref, state_ref):
    @pl.when(pl.program_id(1) == 0)
    def _():
        state_ref[...] = jnp.zeros_like(state_ref)

    C = GDN_CHUNK
    ng = ng_ref[...]
    heads = range(GDN_HEADS)
    lanes = [slice(h * GDN_HEAD_DIM, (h + 1) * GDN_HEAD_DIM) for h in heads]
    for c in range(u_ref.shape[0] // C):
        rows = slice(c * C, (c + 1) * C)
        st = [state_ref[h] for h in heads]
        ws = [_dot(jnp.concatenate([w_ref[rows, lanes[h]], qd_ref[rows, lanes[h]]], 0),
                   st[h].astype(BF16)) for h in heads]
        vnb = [(u_ref[rows, lanes[h]].astype(F32) - ws[h][:C]).astype(BF16) for h in heads]
        for h in heads:
            state_ref[h] = st[h] * gl_ref[c, h:h + 1, :] + _dot_tn(kd_ref[rows, lanes[h]], vnb[h])
        for h in heads:
            o = ws[h][C:] + _dot(qk_ref[c, h], vnb[h])
            o = o * lax.rsqrt(jnp.mean(o * o, -1, keepdims=True) + 1e-6) * ng
            o_ref[rows, lanes[h]] = (o * sz_ref[rows, lanes[h]].astype(F32)).astype(BF16)


def _gated_deltanet(gq, gk, gv, sz, col, norm_g, B, S):
    d = D_MODEL
    C = GDN_CHUNK
    R = min(GDN_GROUP * C, S)
    nc = R // C
    col_arr = col.reshape(B, S, 2 * GDN_HEADS)
    row_arr = col_arr.transpose(0, 2, 1)
    blk = pl.BlockSpec((None, R, d), lambda b, i: (b, i, 0))
    qk_blk = pl.BlockSpec((None, nc, GDN_HEADS, C, C), lambda b, i: (b, i, 0, 0, 0))
    gl_blk = pl.BlockSpec((None, nc, GDN_HEADS, GDN_HEAD_DIM), lambda b, i: (b, i, 0, 0))
    act = jax.ShapeDtypeStruct((B, S, d), BF16)
    u, w, qd, kd, qk, gl = pl.pallas_call(
        _gdn_local_kernel,
        grid=(B, S // R),
        in_specs=[blk, blk, blk,
                  pl.BlockSpec((None, R, 2 * GDN_HEADS), lambda b, i: (b, i, 0)),
                  pl.BlockSpec((None, 2 * GDN_HEADS, R), lambda b, i: (b, 0, i))],
        out_specs=[blk, blk, blk, blk, qk_blk, gl_blk],
        out_shape=[act, act, act, act,
                   jax.ShapeDtypeStruct((B, S // C, GDN_HEADS, C, C), BF16),
                   jax.ShapeDtypeStruct((B, S // C, GDN_HEADS, GDN_HEAD_DIM), F32)],
        compiler_params=_params("arbitrary", "arbitrary"),
    )(gq, gk, gv, col_arr, row_arr)
    out = pl.pallas_call(
        _gdn_scan_kernel,
        grid=(B, S // R),
        in_specs=[blk, blk, blk, blk, qk_blk, gl_blk, blk, _const_spec((1, GDN_HEAD_DIM))],
        out_specs=blk,
        out_shape=act,
        scratch_shapes=[pltpu.VMEM((GDN_HEADS, GDN_HEAD_DIM, GDN_HEAD_DIM), F32)],
        compiler_params=_params("arbitrary", "arbitrary"),
    )(u, w, qd, kd, qk, gl, sz.reshape(B, S, d), norm_g.reshape(1, GDN_HEAD_DIM))
    return out.reshape(B * S, d)


def _mix_kernel(x_ref, od_ref, og_ref, ga_ref, gb_ref, wd_ref, wg_ref, wm_ref, g_ref, b_ref,
                o_ref, *, alpha):
    yd = _dot(od_ref[...], wd_ref[...])
    yg = _dot(og_ref[...], wg_ref[...])
    m = ga_ref[...].astype(F32) * yd + gb_ref[...].astype(F32) * yg
    r = alpha * x_ref[...] + _dot(m.astype(BF16), wm_ref[...])
    o_ref[...] = _layer_norm(r, g_ref[...], b_ref[...])


def _mix(xf, od, og, sga, sgb, w_diff_o, w_gdn_o, w_mix_o, ln_g, ln_b, alpha):
    T, d = xf.shape
    tm = min(TOKEN_TILE, T)
    tok = _token_spec(tm, d)
    wspec = _const_spec((d, d))
    vec = _const_spec((1, d))
    return pl.pallas_call(
        functools.partial(_mix_kernel, alpha=alpha), grid=(T // tm,),
        in_specs=[tok] * 5 + [wspec] * 3 + [vec] * 2,
        out_specs=tok, out_shape=jax.ShapeDtypeStruct((T, d), F32),
        compiler_params=_params("arbitrary"),
    )(xf, od, og, sga, sgb, w_diff_o.astype(BF16), w_gdn_o.astype(BF16), w_mix_o.astype(BF16),
      ln_g.reshape(1, d), ln_b.reshape(1, d))


def _cross_kernel(x_ref, k_ref, v_ref, wq_ref, wo_ref, g_ref, b_ref, wr_ref, br_ref,
                  o_ref, idx_ref, gate_ref, *, alpha):
    half = x_ref.shape[0] // 2
    for part in range(2):
        rows = slice(part * half, (part + 1) * half)
        x = x_ref[rows, :]
        q = (_dot(x.astype(BF16), wq_ref[...]) * (MEM_HEAD_DIM ** -0.5)).astype(BF16)
        outs = []
        for h in range(MEM_HEADS):
            lanes = slice(h * MEM_HEAD_DIM, (h + 1) * MEM_HEAD_DIM)
            s = _dot_nt(q[:, lanes], k_ref[:, lanes])
            p = jnp.exp(s - jnp.max(s, -1, keepdims=True))
            outs.append(_dot(p.astype(BF16), v_ref[:, lanes]) / jnp.sum(p, -1, keepdims=True))
        o = jnp.concatenate(outs, 1).astype(BF16)
        r = alpha * x + _dot(o, wo_ref[...])
        x2 = _layer_norm(r, g_ref[...], b_ref[...])
        o_ref[rows, :] = x2

        logits = _dot(x2.astype(BF16), wr_ref[...]) + br_ref[...]
        lane = lax.broadcasted_iota(I32, logits.shape, 1)
        vals, idxs = [], []
        for _ in range(TOP_K):
            mx = jnp.max(logits, -1, keepdims=True)
            ix = jnp.min(jnp.where(logits == mx, lane, N_EXPERTS), -1, keepdims=True)
            vals.append(mx)
            idxs.append(ix)
            logits = jnp.where(lane == ix, -jnp.inf, logits)
        e = jnp.exp(jnp.concatenate(vals, 1) - vals[0])
        gate_ref[rows, :] = e / jnp.sum(e, -1, keepdims=True)
        idx_ref[rows, :] = jnp.concatenate(idxs, 1)


def _cross_attention(x1, mem, w_cq, w_ck, w_cv, w_co, ln_g, ln_b, w_router, b_router, B, S, alpha):
    T, d = x1.shape
    M = mem.shape[1]
    w_kv = jnp.concatenate([w_ck, w_cv], 1).astype(BF16)
    tmem = min(TOKEN_TILE, B * M)
    kv = pl.pallas_call(
        _proj_plain_kernel, grid=(B * M // tmem,),
        in_specs=[_token_spec(tmem, d), _const_spec((d, 2 * d))],
        out_specs=_token_spec(tmem, 2 * d),
        out_shape=jax.ShapeDtypeStruct((B * M, 2 * d), BF16),
        compiler_params=_params("arbitrary"),
    )(mem.reshape(B * M, d), w_kv).reshape(B, M, 2 * d)

    tm = min(TOKEN_TILE, S)
    nt = S // tm
    tok = pl.BlockSpec((tm, d), lambda b, i: (b * nt + i, 0))
    small = pl.BlockSpec((tm, TOP_K), lambda b, i: (b * nt + i, 0))
    return pl.pallas_call(
        functools.partial(_cross_kernel, alpha=alpha), grid=(B, nt),
        in_specs=[tok,
                  pl.BlockSpec((None, M, d), lambda b, i: (b, 0, 0)),
                  pl.BlockSpec((None, M, d), lambda b, i: (b, 0, 1)),
                  _const_spec((d, d)), _const_spec((d, d)), _const_spec((1, d)), _const_spec((1, d)),
                  _const_spec((d, N_EXPERTS)), _const_spec((1, N_EXPERTS))],
        out_specs=[tok, small, small],
        out_shape=[jax.ShapeDtypeStruct((T, d), F32), jax.ShapeDtypeStruct((T, TOP_K), I32),
                   jax.ShapeDtypeStruct((T, TOP_K), F32)],
        compiler_params=_params("arbitrary", "arbitrary"),
    )(x1, kv, kv, w_cq.astype(BF16), w_co.astype(BF16), ln_g.reshape(1, d), ln_b.reshape(1, d),
      w_router.astype(BF16), b_router.reshape(1, N_EXPERTS))


def _rank_kernel(idx_ref, rank_ref, cnt_ref, run_ref):
    @pl.when(pl.program_id(0) == 0)
    def _():
        run_ref[...] = jnp.zeros_like(run_ref)

    idx = idx_ref[...]
    tm = idx.shape[0]
    lane = lax.broadcasted_iota(I32, (tm, N_EXPERTS), 1)
    hits = [lane == idx[:, k:k + 1] for k in range(TOP_K)]
    onehot = sum(hit.astype(F32) for hit in hits)
    r = lax.broadcasted_iota(I32, (tm, tm), 0)
    c = lax.broadcasted_iota(I32, (tm, tm), 1)
    lower = (r > c).astype(BF16)
    before = _dot(lower, onehot.astype(BF16)) + run_ref[...]
    ranks = [jnp.sum(jnp.where(hit, before, 0.0), -1, keepdims=True) for hit in hits]
    rank_ref[...] = jnp.concatenate(ranks, 1).astype(I32)
    run_ref[...] = run_ref[...] + jnp.sum(onehot, 0, keepdims=True)
    cnt_ref[...] = run_ref[...]


def _dest_kernel(idx_ref, rank_ref, start_ref, dest_ref):
    idx = idx_ref[...]
    lane = lax.broadcasted_iota(I32, (idx.shape[0], N_EXPERTS), 1)
    starts = start_ref[...]
    cols = [jnp.sum(jnp.where(lane == idx[:, k:k + 1], starts, 0), -1, keepdims=True)
            for k in range(TOP_K)]
    dest_ref[...] = rank_ref[...] + jnp.concatenate(cols, 1)


def _row_copy(src_ref, src_row, dst_ref, dst_row, sem):
    return pltpu.make_async_copy(src_ref.at[pl.ds(src_row, 1), :], dst_ref.at[pl.ds(dst_row, 1), :], sem)


def _index_copy(dest_hbm, dest_smem, sem_idx, step, slot, n):
    return pltpu.make_async_copy(dest_hbm.at[pl.ds(pl.multiple_of(step * n, n), n)],
                                 dest_smem.at[pl.ds(pl.multiple_of(slot * n, n), n)], sem_idx.at[slot])


def _dispatch_kernel(zero_rows, dest_hbm, x_ref, xs_hbm, dest_smem, zero_buf, sem_idx, sem_rows):
    tm = x_ref.shape[0]
    n = tm * TOP_K
    i = pl.program_id(0)
    last = pl.num_programs(0) - 1
    slot = i % 2

    @pl.when(i == 0)
    def _():
        _index_copy(dest_hbm, dest_smem, sem_idx, 0, 0, n).start()
        zero_buf[...] = jnp.zeros_like(zero_buf)
        bm = zero_buf.shape[0]

        def zero_tile(t):
            row = pl.multiple_of(jnp.maximum(zero_rows[t], 0), bm)
            return pltpu.make_async_copy(zero_buf, xs_hbm.at[pl.ds(row, bm), :], sem_rows)

        for t in range(zero_rows.shape[0]):
            pl.when(zero_rows[t] >= 0)(lambda t=t: zero_tile(t).start())
        for t in range(zero_rows.shape[0]):
            pl.when(zero_rows[t] >= 0)(lambda t=t: zero_tile(t).wait())

    _index_copy(dest_hbm, dest_smem, sem_idx, i, slot, n).wait()

    @pl.when(i < last)
    def _():
        _index_copy(dest_hbm, dest_smem, sem_idx, i + 1, 1 - slot, n).start()

    base = slot * n

    def issue(jj, _):
        for u in range(ROW_UNROLL):
            j = jj * ROW_UNROLL + u
            tok = jj * (ROW_UNROLL // TOP_K) + u // TOP_K
            _row_copy(x_ref, tok, xs_hbm, dest_smem[base + j], sem_rows).start(priority=u % 2)
        return 0

    lax.fori_loop(0, n // ROW_UNROLL, issue, 0)
    for _ in range(TOP_K):
        pltpu.make_async_copy(x_ref, xs_hbm.at[pl.ds(0, tm), :], sem_rows).wait()


def _expert_kernel(te_ref, tv_ref, xs_ref, w1_ref, b1_ref, w2_ref, b2_ref, ys_ref):
    i = pl.program_id(0)

    @pl.when(tv_ref[i] != 0)
    def _():
        ff = w2_ref.shape[0]
        h = _dot(xs_ref[...].astype(BF16), w1_ref[...]) + b1_ref[...]
        h_gate = jnp.minimum(h[:, :ff], SWIGLU_LIMIT)
        h_up = jnp.clip(h[:, ff:], -SWIGLU_LIMIT, SWIGLU_LIMIT)
        act = h_gate * _sigmoid(SWIGLU_ALPHA * h_gate) * (h_up + 1.0)
        ys_ref[...] = _dot(act.astype(BF16), w2_ref[...]) + b2_ref[...]

    @pl.when(tv_ref[i] == 0)
    def _():
        ys_ref[...] = jnp.zeros_like(ys_ref)


def _combine_kernel(dest_hbm, ys_hbm, x_ref, gate_ref, g_ref, b_ref, o_ref,
                    dest_smem, buf, sem_idx, sem_rows, *, alpha):
    tm = x_ref.shape[0]
    n = tm * TOP_K
    i = pl.program_id(0)
    last = pl.num_programs(0) - 1
    slot = i % 2

    def index_copy(step):
        return _index_copy(dest_hbm, dest_smem, sem_idx, step, step % 3, n)

    def issue_rows(step):
        s = step % 2
        base = (step % 3) * n

        def issue(jj, _):
            for u in range(ROW_UNROLL):
                j = jj * ROW_UNROLL + u
                tok = jj * (ROW_UNROLL // TOP_K) + u // TOP_K
                _row_copy(ys_hbm, dest_smem[base + j], buf.at[s, u % TOP_K], tok,
                          sem_rows.at[s]).start(priority=u % 2)
            return 0

        lax.fori_loop(0, n // ROW_UNROLL, issue, 0)

    @pl.when(i == 0)
    def _():
        index_copy(0).start()
        index_copy(0).wait()

    @pl.when((i == 0) & (last > 0))
    def _():
        index_copy(1).start()

    @pl.when(i == 0)
    def _():
        issue_rows(0)

    @pl.when(i < last)
    def _():
        index_copy(i + 1).wait()

    @pl.when(i + 1 < last)
    def _():
        index_copy(i + 2).start()

    @pl.when(i < last)
    def _():
        issue_rows(i + 1)

    for k in range(TOP_K):
        pltpu.make_async_copy(ys_hbm.at[pl.ds(0, tm), :], buf.at[slot, k], sem_rows.at[slot]).wait()
    gates = gate_ref[...]
    y = sum(gates[:, k:k + 1] * buf[slot, k] for k in range(TOP_K))
    o_ref[...] = _layer_norm(alpha * x_ref[...] + y, g_ref[...], b_ref[...])


def _moe(x2, idx, gates, w1, b1, w2, b2, ln_g, ln_b, alpha):
    T, d = x2.shape
    ff = w2.shape[1]
    tk = T * TOP_K
    bm = EXPERT_TILE
    n_tiles = -(-tk // bm) + N_EXPERTS
    R = n_tiles * bm

    tr = min(TOKEN_TILE, T)
    small = lambda tm: pl.BlockSpec((tm, TOP_K), lambda i: (i, 0))
    rank, counts = pl.pallas_call(
        _rank_kernel, grid=(T // tr,),
        in_specs=[small(tr)], out_specs=[small(tr), _const_spec((1, N_EXPERTS))],
        out_shape=[jax.ShapeDtypeStruct((T, TOP_K), I32), jax.ShapeDtypeStruct((1, N_EXPERTS), F32)],
        scratch_shapes=[pltpu.VMEM((1, N_EXPERTS), F32)],
        compiler_params=_params("arbitrary"),
    )(idx)

    counts = counts.reshape(N_EXPERTS).astype(I32)
    padded = (counts + bm - 1) // bm * bm
    pad_ends = jnp.cumsum(padded)
    pad_starts = pad_ends - padded
    tile_row = jnp.arange(n_tiles, dtype=I32) * bm
    tile_e = jnp.minimum(jnp.sum(pad_ends[None, :] <= tile_row[:, None], 1), N_EXPERTS - 1).astype(I32)
    tile_valid = (tile_row < pad_ends[-1]).astype(I32)

    dest = pl.pallas_call(
        _dest_kernel, grid=(T // tr,),
        in_specs=[small(tr), small(tr), _const_spec((1, N_EXPERTS))], out_specs=small(tr),
        out_shape=jax.ShapeDtypeStruct((T, TOP_K), I32),
        compiler_params=_params("arbitrary"),
    )(idx, rank, pad_starts.reshape(1, N_EXPERTS)).reshape(tk)

    tm = min(ROUTE_TILE, T)
    n = tm * TOP_K
    any_spec = pl.BlockSpec(memory_space=pl.ANY)
    last_tile = jnp.where(padded > 0, pad_ends - bm, -1)
    tail = pad_ends[-1] + jnp.arange(n_tiles - tk // bm, dtype=I32) * bm
    tail = jnp.where(tail < R, tail, -1)
    zero_rows = jnp.concatenate([last_tile, tail]).astype(I32)
    xs = pl.pallas_call(
        _dispatch_kernel,
        grid_spec=pltpu.PrefetchScalarGridSpec(
            num_scalar_prefetch=1, grid=(T // tm,),
            in_specs=[any_spec, pl.BlockSpec((tm, d), lambda i, zr: (i, 0))], out_specs=any_spec,
            scratch_shapes=[pltpu.SMEM((2 * n,), I32), pltpu.VMEM((bm, d), F32),
                            pltpu.SemaphoreType.DMA((2,)), pltpu.SemaphoreType.DMA(())]),
        out_shape=jax.ShapeDtypeStruct((R, d), F32),
        compiler_params=_params("arbitrary"),
    )(zero_rows, dest, x2)

    ys = pl.pallas_call(
        _expert_kernel,
        grid_spec=pltpu.PrefetchScalarGridSpec(
            num_scalar_prefetch=2, grid=(n_tiles,),
            in_specs=[pl.BlockSpec((bm, d), lambda i, te, tv: (i, 0)),
                      pl.BlockSpec((None, d, 2 * ff), lambda i, te, tv: (te[i], 0, 0)),
                      pl.BlockSpec((None, 1, 2 * ff), lambda i, te, tv: (te[i], 0, 0)),
                      pl.BlockSpec((None, ff, d), lambda i, te, tv: (te[i], 0, 0)),
                      pl.BlockSpec((None, 1, d), lambda i, te, tv: (te[i], 0, 0))],
            out_specs=pl.BlockSpec((bm, d), lambda i, te, tv: (i, 0))),
        out_shape=jax.ShapeDtypeStruct((R, d), F32),
        compiler_params=_params("arbitrary"),
    )(tile_e, tile_valid, xs, w1.astype(BF16), b1.reshape(N_EXPERTS, 1, 2 * ff),
      w2.astype(BF16), b2.reshape(N_EXPERTS, 1, d))

    return pl.pallas_call(
        functools.partial(_combine_kernel, alpha=alpha), grid=(T // tm,),
        in_specs=[any_spec, any_spec, _token_spec(tm, d), small(tm), _const_spec((1, d)), _const_spec((1, d))],
        out_specs=_token_spec(tm, d),
        out_shape=jax.ShapeDtypeStruct((T, d), F32),
        scratch_shapes=[pltpu.SMEM((3 * n,), I32), pltpu.VMEM((2, TOP_K, tm, d), F32),
                        pltpu.SemaphoreType.DMA((3,)), pltpu.SemaphoreType.DMA((2,))],
        compiler_params=_params("arbitrary"),
    )(dest, ys, x2, gates, ln_g.reshape(1, d), ln_b.reshape(1, d))


def kernel(x, mem, positions, w_in, diff_lambda_q1, diff_lambda_k1, diff_lambda_q2, diff_lambda_k2, diff_subln_g, w_diff_o, gdn_conv_w, gdn_A_log, gdn_dt_bias, gdn_norm_g, w_gdn_o, w_mix_o, ln1_g, ln1_b, w_cq, w_ck, w_cv, w_co, ln2_g, ln2_b, w_router, b_router, w_exp_in, b_exp_in, w_exp_out, b_exp_out, ln3_g, ln3_b):
    B, S, d = x.shape
    depth = w_in.shape[0]
    alpha = (2 * depth) ** 0.25
    rope = _rope_tables(positions)
    xf = x.reshape(B * S, d)
    for l in range(depth):
        lambda_init = 0.8 - 0.6 * math.exp(-0.3 * l)
        dq, dk, dvt, (gq, gk, gv), sz, sga, sgb, gdn_col = _input_projections(
            xf, w_in[l], rope, gdn_conv_w[l], gdn_A_log[l], gdn_dt_bias[l], B, S)
        lam_vecs = jnp.stack([diff_lambda_q1[l], diff_lambda_k1[l], diff_lambda_q2[l], diff_lambda_k2[l]])
        od = _diff_attention(dq, dk, dvt, lam_vecs, diff_subln_g[l], B, S, lambda_init)
        og = _gated_deltanet(gq, gk, gv, sz, gdn_col, gdn_norm_g[l], B, S)
        x1 = _mix(xf, od, og, sga, sgb, w_diff_o[l], w_gdn_o[l], w_mix_o[l], ln1_g[l], ln1_b[l], alpha)
        x2, idx, gates = _cross_attention(x1, mem, w_cq[l], w_ck[l], w_cv[l], w_co[l], ln2_g[l], ln2_b[l],
                                          w_router[l], b_router[l], B, S, alpha)
        xf = _moe(x2, idx, gates, w_exp_in[l], b_exp_in[l], w_exp_out[l], b_exp_out[l],
                  ln3_g[l], ln3_b[l], alpha)
    return xf.reshape(B, S, d)
```

```python
import functools
import math

import jax
import jax.numpy as jnp
from jax import lax
from jax.experimental import pallas as pl
from jax.experimental.pallas import tpu as pltpu

F32 = jnp.float32
BF16 = jnp.bfloat16
I32 = jnp.int32

D_MODEL = 1024
DIFF_HEADS = 8
DIFF_HEAD_DIM = 64
ROPE_THETA = 500000.0
ROT_DIM = DIFF_HEAD_DIM // 4
ROT_HALF = ROT_DIM // 2
GDN_HEADS = 8
GDN_HEAD_DIM = 128
CONV_WIDTH = 4
GDN_CHUNK = 64
MEM_HEADS = 4
MEM_HEAD_DIM = D_MODEL // MEM_HEADS
N_EXPERTS = 32
TOP_K = 4
SWIGLU_LIMIT = 7.0
SWIGLU_ALPHA = 1.702
LANES = 128
SUBLANES = 8
VMEM_LIMIT = 48 * 1024 * 1024

TOKEN_TILE = 1024
ATTN_Q_TILE = 512
ATTN_K_TILE = 512
GDN_GROUP = 4
ROUTE_TILE = 512
EXPERT_TILE = 512
ROW_UNROLL = 32


def _params(*sem):
    return pltpu.CompilerParams(dimension_semantics=sem, vmem_limit_bytes=VMEM_LIMIT)


def _dot(a, b):
    return jnp.dot(a, b, preferred_element_type=F32)


def _dot_nt(a, b):
    return lax.dot_general(a, b, (((1,), (1,)), ((), ())), preferred_element_type=F32)


def _dot_tn(a, b):
    return lax.dot_general(a, b, (((0,), (0,)), ((), ())), preferred_element_type=F32)


def _sigmoid(x):
    return 1.0 / (1.0 + jnp.exp(-x))


def _layer_norm(r, g, b):
    mu = jnp.mean(r, -1, keepdims=True)
    d = r - mu
    var = jnp.mean(d * d, -1, keepdims=True)
    return d * lax.rsqrt(var + 1e-5) * g + b


def _rope_kernel(pos_ref, inv_ref, cos_ref, sa_ref, sb_ref):
    ang = pos_ref[...] * inv_ref[...]
    sin = jnp.sin(ang)
    first_half = lax.broadcasted_iota(I32, ang.shape, 1) % DIFF_HEAD_DIM < ROT_HALF
    cos_ref[...] = jnp.cos(ang)
    sa_ref[...] = jnp.where(first_half, -sin, 0.0)
    sb_ref[...] = jnp.where(first_half, 0.0, sin)


def _rope_tables(positions):
    T = positions.size
    inv_freq = jnp.power(ROPE_THETA, -jnp.arange(0, ROT_DIM, 2, dtype=F32) / ROT_DIM)
    inv = jnp.concatenate([inv_freq, inv_freq, jnp.zeros((DIFF_HEAD_DIM - ROT_DIM,), F32)])
    inv = jnp.tile(inv, LANES // DIFF_HEAD_DIM).reshape(1, LANES)
    pos = jnp.broadcast_to(positions.reshape(T, 1).astype(F32), (T, LANES))
    rb = min(T, 1024)
    spec = pl.BlockSpec((rb, LANES), lambda i: (i, 0))
    return pl.pallas_call(
        _rope_kernel,
        grid=(T // rb,),
        in_specs=[spec, _const_spec((1, LANES))],
        out_specs=[spec] * 3,
        out_shape=[jax.ShapeDtypeStruct((T, LANES), F32)] * 3,
        compiler_params=_params("arbitrary"),
    )(pos, inv)


def _proj_diff_kernel(x_ref, w_ref, wvt_ref, cos_ref, sa_ref, sb_ref, q_ref, k_ref, vt_ref):
    xb = x_ref[...].astype(BF16)
    cos_t, sin_a, sin_b = cos_ref[...], sa_ref[...], sb_ref[...]
    qk_w = DIFF_HEADS * 2 * DIFF_HEAD_DIM
    for part, o_ref in enumerate((q_ref, k_ref)):
        acc = _dot(xb, w_ref[:, part * qk_w:(part + 1) * qk_w])
        for h in range(DIFF_HEADS):
            t = acc[:, h * LANES:(h + 1) * LANES]
            r = (t * cos_t + pltpu.roll(t, LANES - ROT_HALF, 1) * sin_a
                 + pltpu.roll(t, ROT_HALF, 1) * sin_b)
            if part == 0:
                r = r * (DIFF_HEAD_DIM ** -0.5 * math.log2(math.e))
            o_ref[:, h * LANES:(h + 1) * LANES] = r.astype(BF16)
    vt = _dot_nt(wvt_ref[...], xb)
    tk = vt_ref.shape[-1]
    for h in range(DIFF_HEADS):
        for c in range(vt_ref.shape[1]):
            vt_ref[h, c] = vt[h * LANES:(h + 1) * LANES, c * tk:(c + 1) * tk].astype(BF16)


def _proj_plain_kernel(x_ref, w_ref, o_ref):
    o_ref[...] = _dot(x_ref[...].astype(BF16), w_ref[...]).astype(BF16)


def _proj_gdn_kernel(x_ref, w_ref, cw_ref, q_ref, k_ref, v_ref, tail_ref, *, tiles_per_seq):
    xb = x_ref[...].astype(BF16)
    tm = xb.shape[0]
    d = q_ref.shape[1]
    first = pl.program_id(0) % tiles_per_seq == 0
    row8 = lax.broadcasted_iota(I32, (SUBLANES, d), 0)
    for part, o_ref in enumerate((q_ref, k_ref, v_ref)):
        cols = slice(part * d, (part + 1) * d)
        acc = _dot(xb, w_ref[:, cols])
        prev = jnp.where(first, 0.0, tail_ref[:, cols])
        tail_ref[:, cols] = acc[tm - SUBLANES:, :]
        y = acc * cw_ref[CONV_WIDTH - 1:CONV_WIDTH, cols]
        head = y[:SUBLANES]
        for j in range(1, CONV_WIDTH):
            cw_j = cw_ref[CONV_WIDTH - 1 - j:CONV_WIDTH - j, cols]
            shifted = pltpu.roll(acc, j, 0)
            y = y + shifted * cw_j
            head = head + jnp.where(row8 >= j, shifted[:SUBLANES], pltpu.roll(prev, j, 0)) * cw_j
        y = jnp.concatenate([head, y[SUBLANES:]], 0)
        y = y * _sigmoid(y)
        if part == 2:
            o_ref[...] = y.astype(BF16)
            continue
        for h in range(d // GDN_HEAD_DIM):
            lanes = slice(h * GDN_HEAD_DIM, (h + 1) * GDN_HEAD_DIM)
            yh = y[:, lanes]
            yh = yh * lax.rsqrt(jnp.sum(yh * yh, -1, keepdims=True) + 1e-6)
            if part == 0:
                yh = yh * (GDN_HEAD_DIM ** -0.5)
            o_ref[:, lanes] = yh.astype(BF16)


def _proj_gate_kernel(x_ref, w_ref, wba_ref, alog_ref, dtb_ref, sz_ref, ga_ref, gb_ref, col_ref):
    xb = x_ref[...].astype(BF16)
    z = _dot(xb, w_ref[:, :D_MODEL])
    sz_ref[...] = (z * _sigmoid(z)).astype(BF16)
    ga_ref[...] = _sigmoid(_dot(xb, w_ref[:, D_MODEL:2 * D_MODEL])).astype(BF16)
    gb_ref[...] = _sigmoid(_dot(xb, w_ref[:, 2 * D_MODEL:])).astype(BF16)

    ba = _dot(xb, wba_ref[...])
    lane = lax.broadcasted_iota(I32, ba.shape, 1)
    row = lax.broadcasted_iota(I32, ba.shape, 0)
    s = ba + dtb_ref[...]
    softplus = jnp.maximum(s, 0.0) + jnp.log1p(jnp.exp(-jnp.abs(s)))
    g = -jnp.exp(alog_ref[...]) * softplus
    shift = 1
    while shift < GDN_CHUNK:
        g = g + jnp.where(row % GDN_CHUNK >= shift, pltpu.roll(g, shift, 0), 0.0)
        shift *= 2
    col_ref[...] = jnp.where(lane < GDN_HEADS, _sigmoid(ba), g)


def _token_spec(tm, width):
    return pl.BlockSpec((tm, width), lambda i: (i, 0))


def _const_spec(shape):
    return pl.BlockSpec(shape, lambda *_: (0,) * len(shape))


def _input_projections(xf, w_in, rope, conv_w, a_log, dt_bias, B, S):
    T = xf.shape[0]
    tm = min(TOKEN_TILE, S)
    grid = (T // tm,)
    d = D_MODEL
    w_diff = w_in[:, :2 * d].astype(BF16)
    w_vt = w_in[:, 2 * d:3 * d].T.astype(BF16)
    w_gdn = w_in[:, 3 * d:6 * d].astype(BF16)
    w_z = w_in[:, 6 * d:7 * d]
    w_ba = w_in[:, 7 * d:7 * d + 2 * GDN_HEADS].astype(BF16)
    w_gate = jnp.concatenate([w_z, w_in[:, 7 * d + 2 * GDN_HEADS:]], 1).astype(BF16)
    x_spec = _token_spec(tm, d)
    act = jax.ShapeDtypeStruct((T, d), BF16)

    tk = min(ATTN_K_TILE, S)
    nt = S // tm
    dq, dk, dvt = pl.pallas_call(
        _proj_diff_kernel, grid=grid,
        in_specs=[x_spec, _const_spec((d, 2 * d)), _const_spec((d, d))] + [_token_spec(tm, LANES)] * 3,
        out_specs=[_token_spec(tm, d), _token_spec(tm, d),
                   pl.BlockSpec((None, DIFF_HEADS, tm // tk, LANES, tk),
                                lambda i: (i // nt, 0, i % nt, 0, 0))],
        out_shape=[act, act, jax.ShapeDtypeStruct((B, DIFF_HEADS, S // tk, LANES, tk), BF16)],
        compiler_params=_params("arbitrary"),
    )(xf, w_diff, w_vt, *rope)

    gq, gk, gv = pl.pallas_call(
        functools.partial(_proj_gdn_kernel, tiles_per_seq=nt), grid=grid,
        in_specs=[x_spec, _const_spec((d, 3 * d)), _const_spec((CONV_WIDTH, 3 * d))],
        out_specs=[_token_spec(tm, d)] * 3, out_shape=[act] * 3,
        scratch_shapes=[pltpu.VMEM((SUBLANES, 3 * d), F32)],
        compiler_params=_params("arbitrary"),
    )(xf, w_gdn, conv_w)

    nba = 2 * GDN_HEADS
    zeros = jnp.zeros((GDN_HEADS,), F32)
    sz, sga, sgb, col = pl.pallas_call(
        _proj_gate_kernel, grid=grid,
        in_specs=[x_spec, _const_spec((d, 3 * d)), _const_spec((d, nba)),
                  _const_spec((1, nba)), _const_spec((1, nba))],
        out_specs=[_token_spec(tm, d)] * 3 + [_token_spec(tm, nba)],
        out_shape=[act] * 3 + [jax.ShapeDtypeStruct((T, nba), F32)],
        compiler_params=_params("arbitrary"),
    )(xf, w_gate, w_ba, jnp.concatenate([zeros, a_log]).reshape(1, nba),
      jnp.concatenate([zeros, dt_bias]).reshape(1, nba))
    gdn_qkv = tuple(t.reshape(B, S, d) for t in (gq, gk, gv))
    return dq, dk, dvt, gdn_qkv, sz, sga, sgb, col


def _diff_attn_kernel(lam_ref, g_ref, q_ref, k_ref, vt_ref, o_ref,
                      sa_ref, sb_ref, m_ref, l_ref, acc_ref, *, lambda_init, tq):
    tk = vt_ref.shape[-1]
    lv = lam_ref[...]
    lam = (jnp.exp(jnp.sum(lv[0:1] * lv[1:2], -1, keepdims=True))
           - jnp.exp(jnp.sum(lv[2:3] * lv[3:4], -1, keepdims=True)) + lambda_init)
    lane = lax.broadcasted_iota(I32, (tq, LANES), 1)
    key = lax.broadcasted_iota(I32, (tk, 2 * tq), 0)
    qry = lax.broadcasted_iota(I32, (tk, 2 * tq), 1)
    key_minus_qry = key - jnp.where(qry >= tq, qry - tq, qry)
    bufs = (sa_ref, sb_ref)

    for qi in range(q_ref.shape[0] // tq):
        q = q_ref[qi * tq:(qi + 1) * tq, :]
        zero = jnp.zeros_like(q)
        q2 = jnp.concatenate([jnp.where(lane < DIFF_HEAD_DIM, q, zero),
                              jnp.where(lane >= DIFF_HEAD_DIM, q, zero)], 0)
        last = (qi * tq) // tk

        def scores(j):
            return _dot_nt(k_ref[j * tk:(j + 1) * tk, :], q2)

        def update(s, j):
            m = m_ref[...]
            m_new = jnp.maximum(m, jnp.max(s, 0, keepdims=True))
            alpha = jnp.exp2(m - m_new)
            p = jnp.exp2(s - m_new)
            l_ref[...] = alpha * l_ref[...] + jnp.sum(p, 0, keepdims=True)
            acc_ref[...] = alpha * acc_ref[...] + _dot(vt_ref[j], p.astype(BF16))
            m_ref[...] = m_new

        m_ref[...] = jnp.full(m_ref.shape, -jnp.inf, F32)
        l_ref[...] = jnp.zeros_like(l_ref)
        acc_ref[...] = jnp.zeros_like(acc_ref)
        bufs[0][...] = scores(0)
        for j in range(last + 1):
            if j < last:
                bufs[(j + 1) % 2][...] = scores(j + 1)
            s = bufs[j % 2][...]
            if j == last:
                s = jnp.where(key_minus_qry <= qi * tq - last * tk, s, -jnp.inf)
            update(s, j)

        o = acc_ref[...] / l_ref[...]
        o = o[:, :tq] - lam * o[:, tq:]
        o = o * lax.rsqrt(jnp.mean(o * o, 0, keepdims=True) + 1e-6) * g_ref[...]
        o_ref[qi * tq:(qi + 1) * tq, :] = (o * (1.0 - lambda_init)).T.astype(BF16)


def _diff_attention(dq, dk, dvt, lam_vecs, subln_g, B, S, lambda_init):
    tq = min(ATTN_Q_TILE, S)
    tk = dvt.shape[-1]
    d = D_MODEL
    q3, k3 = dq.reshape(B, S, d), dk.reshape(B, S, d)
    seq = pl.BlockSpec((None, S, LANES), lambda b, h: (b, 0, h))
    out = pl.pallas_call(
        functools.partial(_diff_attn_kernel, lambda_init=lambda_init, tq=tq),
        grid=(B, DIFF_HEADS),
        in_specs=[_const_spec((4, DIFF_HEAD_DIM)), _const_spec((LANES, 1)), seq, seq,
                  pl.BlockSpec((None, None, S // tk, LANES, tk), lambda b, h: (b, h, 0, 0, 0))],
        out_specs=seq,
        out_shape=jax.ShapeDtypeStruct((B, S, d), BF16),
        scratch_shapes=[pltpu.VMEM((tk, 2 * tq), F32), pltpu.VMEM((tk, 2 * tq), F32),
                        pltpu.VMEM((1, 2 * tq), F32), pltpu.VMEM((1, 2 * tq), F32),
                        pltpu.VMEM((LANES, 2 * tq), F32)],
        compiler_params=_params("arbitrary", "arbitrary"),
    )(lam_vecs, subln_g.reshape(LANES, 1), q3, k3, dvt)
    return out.reshape(B * S, d)


def _gdn_local_kernel(q_ref, k_ref, v_ref, col_ref, row_ref,
                      u_ref, w_ref, qd_ref, kd_ref, qk_ref, gl_ref):
    C = GDN_CHUNK
    R = q_ref.shape[0]
    ri = lax.broadcasted_iota(I32, (R, R), 0)
    ci = lax.broadcasted_iota(I32, (R, R), 1)
    same = (ri // C) == (ci // C)
    incl = same & (ri >= ci)
    strict = same & (ri > ci)
    eye = (ri == ci).astype(F32)
    heads = range(GDN_HEADS)
    head_lanes = [slice(h * GDN_HEAD_DIM, (h + 1) * GDN_HEAD_DIM) for h in heads]
    decays, pbs, t_mats = [], [], []
    for h in heads:
        kh = k_ref[:, head_lanes[h]]
        g_c = col_ref[:, GDN_HEADS + h:GDN_HEADS + h + 1]
        g_r = row_ref[GDN_HEADS + h:GDN_HEADS + h + 1, :]
        decay = jnp.where(incl, jnp.exp(jnp.where(incl, g_c - g_r, 0.0)), 0.0)
        p = -(jnp.where(strict, _dot_nt(kh, kh) * decay, 0.0) * col_ref[:, h:h + 1])
        decays.append(decay)
        pbs.append(p.astype(BF16))
        t_mats.append(eye + p)
    for _ in range(5):
        pbs = [_dot(pb, pb).astype(BF16) for pb in pbs]
        t_mats = [t + _dot(t.astype(BF16), pb) for t, pb in zip(t_mats, pbs)]
    for h in heads:
        lanes = head_lanes[h]
        qh, kh, vh = q_ref[:, lanes], k_ref[:, lanes], v_ref[:, lanes]
        beta_c = col_ref[:, h:h + 1]
        g_c = col_ref[:, GDN_HEADS + h:GDN_HEADS + h + 1]
        decay = decays[h]
        eg_c = jnp.exp(g_c)
        kf = kh.astype(F32)
        rhs = jnp.concatenate([vh.astype(F32) * beta_c, kf * (beta_c * eg_c)], 1)
        uw = _dot(t_mats[h].astype(BF16), rhs.astype(BF16))
        u_ref[:, lanes] = uw[:, :GDN_HEAD_DIM].astype(BF16)
        w_ref[:, lanes] = uw[:, GDN_HEAD_DIM:].astype(BF16)
        qd_ref[:, lanes] = (qh.astype(F32) * eg_c).astype(BF16)
        for c in range(R // C):
            rows = slice(c * C, (c + 1) * C)
            g_last = g_c[(c + 1) * C - 1:(c + 1) * C, :]
            kd_ref[rows, lanes] = (kf[rows] * jnp.exp(g_last - g_c[rows])).astype(BF16)
            gl_ref[c, h:h + 1, :] = jnp.broadcast_to(jnp.exp(g_last), (1, GDN_HEAD_DIM))
            qk_ref[c, h] = (_dot_nt(qh[rows], kh[rows]) * decay[rows, rows]).astype(BF16)


def _gdn_scan_kernel(u_ref, w_ref, qd_ref, kd_ref, qk_ref, gl_ref, sz_ref, ng_ref, o_ref, state_ref):
    @pl.when(pl.program_id(1) == 0)
    def _():
        state_ref[...] = jnp.zeros_like(state_ref)

    C = GDN_CHUNK
    ng = ng_ref[...]
    heads = range(GDN_HEADS)
    lanes = [slice(h * GDN_HEAD_DIM, (h + 1) * GDN_HEAD_DIM) for h in heads]
    for c in range(u_ref.shape[0] // C):
        rows = slice(c * C, (c + 1) * C)
        st = [state_ref[h] for h in heads]
        ws = [_dot(jnp.concatenate([w_ref[rows, lanes[h]], qd_ref[rows, lanes[h]]], 0),
                   st[h].astype(BF16)) for h in heads]
        vnb = [(u_ref[rows, lanes[h]].astype(F32) - ws[h][:C]).astype(BF16) for h in heads]
        for h in heads:
            state_ref[h] = st[h] * gl_ref[c, h:h + 1, :] + _dot_tn(kd_ref[rows, lanes[h]], vnb[h])
        for h in heads:
            o = ws[h][C:] + _dot(qk_ref[c, h], vnb[h])
            o = o * lax.rsqrt(jnp.mean(o * o, -1, keepdims=True) + 1e-6) * ng
            o_ref[rows, lanes[h]] = (o * sz_ref[rows, lanes[h]].astype(F32)).astype(BF16)


def _gated_deltanet(gq, gk, gv, sz, col, norm_g, B, S):
    d = D_MODEL
    C = GDN_CHUNK
    R = min(GDN_GROUP * C, S)
    nc = R // C
    col_arr = col.reshape(B, S, 2 * GDN_HEADS)
    row_arr = col_arr.transpose(0, 2, 1)
    blk = pl.BlockSpec((None, R, d), lambda b, i: (b, i, 0))
    qk_blk = pl.BlockSpec((None, nc, GDN_HEADS, C, C), lambda b, i: (b, i, 0, 0, 0))
    gl_blk = pl.BlockSpec((None, nc, GDN_HEADS, GDN_HEAD_DIM), lambda b, i: (b, i, 0, 0))
    act = jax.ShapeDtypeStruct((B, S, d), BF16)
    u, w, qd, kd, qk, gl = pl.pallas_call(
        _gdn_local_kernel,
        grid=(B, S // R),
        in_specs=[blk, blk, blk,
                  pl.BlockSpec((None, R, 2 * GDN_HEADS), lambda b, i: (b, i, 0)),
                  pl.BlockSpec((None, 2 * GDN_HEADS, R), lambda b, i: (b, 0, i))],
        out_specs=[blk, blk, blk, blk, qk_blk, gl_blk],
        out_shape=[act, act, act, act,
                   jax.ShapeDtypeStruct((B, S // C, GDN_HEADS, C, C), BF16),
                   jax.ShapeDtypeStruct((B, S // C, GDN_HEADS, GDN_HEAD_DIM), F32)],
        compiler_params=_params("arbitrary", "arbitrary"),
    )(gq, gk, gv, col_arr, row_arr)
    R2 = min(2 * R, S)
    blk = pl.BlockSpec((None, R2, d), lambda b, i: (b, i, 0))
    qk_blk = pl.BlockSpec((None, R2 // C, GDN_HEADS, C, C), lambda b, i: (b, i, 0, 0, 0))
    gl_blk = pl.BlockSpec((None, R2 // C, GDN_HEADS, GDN_HEAD_DIM), lambda b, i: (b, i, 0, 0))
    out = pl.pallas_call(
        _gdn_scan_kernel,
        grid=(B, S // R2),
        in_specs=[blk, blk, blk, blk, qk_blk, gl_blk, blk, _const_spec((1, GDN_HEAD_DIM))],
        out_specs=blk,
        out_shape=act,
        scratch_shapes=[pltpu.VMEM((GDN_HEADS, GDN_HEAD_DIM, GDN_HEAD_DIM), F32)],
        compiler_params=_params("arbitrary", "arbitrary"),
    )(u, w, qd, kd, qk, gl, sz.reshape(B, S, d), norm_g.reshape(1, GDN_HEAD_DIM))
    return out.reshape(B * S, d)


def _mix_kernel(x_ref, od_ref, og_ref, ga_ref, gb_ref, wd_ref, wg_ref, wm_ref, g_ref, b_ref,
                o_ref, *, alpha):
    yd = _dot(od_ref[...], wd_ref[...])
    yg = _dot(og_ref[...], wg_ref[...])
    m = ga_ref[...].astype(F32) * yd + gb_ref[...].astype(F32) * yg
    r = alpha * x_ref[...] + _dot(m.astype(BF16), wm_ref[...])
    o_ref[...] = _layer_norm(r, g_ref[...], b_ref[...])


def _mix(xf, od, og, sga, sgb, w_diff_o, w_gdn_o, w_mix_o, ln_g, ln_b, alpha):
    T, d = xf.shape
    tm = min(TOKEN_TILE, T)
    tok = _token_spec(tm, d)
    wspec = _const_spec((d, d))
    vec = _const_spec((1, d))
    return pl.pallas_call(
        functools.partial(_mix_kernel, alpha=alpha), grid=(T // tm,),
        in_specs=[tok] * 5 + [wspec] * 3 + [vec] * 2,
        out_specs=tok, out_shape=jax.ShapeDtypeStruct((T, d), F32),
        compiler_params=_params("arbitrary"),
    )(xf, od, og, sga, sgb, w_diff_o.astype(BF16), w_gdn_o.astype(BF16), w_mix_o.astype(BF16),
      ln_g.reshape(1, d), ln_b.reshape(1, d))


def _cross_kernel(x_ref, k_ref, v_ref, wq_ref, wo_ref, g_ref, b_ref, wr_ref, br_ref,
                  o_ref, idx_ref, gate_ref, *, alpha):
    half = x_ref.shape[0] // 2
    for part in range(2):
        rows = slice(part * half, (part + 1) * half)
        x = x_ref[rows, :]
        q = (_dot(x.astype(BF16), wq_ref[...]) * (MEM_HEAD_DIM ** -0.5)).astype(BF16)
        outs = []
        for h in range(MEM_HEADS):
            lanes = slice(h * MEM_HEAD_DIM, (h + 1) * MEM_HEAD_DIM)
            s = _dot_nt(q[:, lanes], k_ref[:, lanes])
            p = jnp.exp(s - jnp.max(s, -1, keepdims=True))
            outs.append(_dot(p.astype(BF16), v_ref[:, lanes]) / jnp.sum(p, -1, keepdims=True))
        o = jnp.concatenate(outs, 1).astype(BF16)
        r = alpha * x + _dot(o, wo_ref[...])
        x2 = _layer_norm(r, g_ref[...], b_ref[...])
        o_ref[rows, :] = x2

        logits = _dot(x2.astype(BF16), wr_ref[...]) + br_ref[...]
        lane = lax.broadcasted_iota(I32, logits.shape, 1)
        vals, idxs = [], []
        for _ in range(TOP_K):
            mx = jnp.max(logits, -1, keepdims=True)
            ix = jnp.min(jnp.where(logits == mx, lane, N_EXPERTS), -1, keepdims=True)
            vals.append(mx)
            idxs.append(ix)
            logits = jnp.where(lane == ix, -jnp.inf, logits)
        e = jnp.exp(jnp.concatenate(vals, 1) - vals[0])
        gate_ref[rows, :] = e / jnp.sum(e, -1, keepdims=True)
        idx_ref[rows, :] = jnp.concatenate(idxs, 1)


def _cross_attention(x1, mem, w_cq, w_ck, w_cv, w_co, ln_g, ln_b, w_router, b_router, B, S, alpha):
    T, d = x1.shape
    M = mem.shape[1]
    w_kv = jnp.concatenate([w_ck, w_cv], 1).astype(BF16)
    tmem = min(TOKEN_TILE, B * M)
    kv = pl.pallas_call(
        _proj_plain_kernel, grid=(B * M // tmem,),
        in_specs=[_token_spec(tmem, d), _const_spec((d, 2 * d))],
        out_specs=_token_spec(tmem, 2 * d),
        out_shape=jax.ShapeDtypeStruct((B * M, 2 * d), BF16),
        compiler_params=_params("arbitrary"),
    )(mem.reshape(B * M, d), w_kv).reshape(B, M, 2 * d)

    tm = min(TOKEN_TILE, S)
    nt = S // tm
    tok = pl.BlockSpec((tm, d), lambda b, i: (b * nt + i, 0))
    small = pl.BlockSpec((tm, TOP_K), lambda b, i: (b * nt + i, 0))
    return pl.pallas_call(
        functools.partial(_cross_kernel, alpha=alpha), grid=(B, nt),
        in_specs=[tok,
                  pl.BlockSpec((None, M, d), lambda b, i: (b, 0, 0)),
                  pl.BlockSpec((None, M, d), lambda b, i: (b, 0, 1)),
                  _const_spec((d, d)), _const_spec((d, d)), _const_spec((1, d)), _const_spec((1, d)),
                  _const_spec((d, N_EXPERTS)), _const_spec((1, N_EXPERTS))],
        out_specs=[tok, small, small],
        out_shape=[jax.ShapeDtypeStruct((T, d), F32), jax.ShapeDtypeStruct((T, TOP_K), I32),
                   jax.ShapeDtypeStruct((T, TOP_K), F32)],
        compiler_params=_params("arbitrary", "arbitrary"),
    )(x1, kv, kv, w_cq.astype(BF16), w_co.astype(BF16), ln_g.reshape(1, d), ln_b.reshape(1, d),
      w_router.astype(BF16), b_router.reshape(1, N_EXPERTS))


def _rank_kernel(idx_ref, rank_ref, cnt_ref, run_ref):
    @pl.when(pl.program_id(0) == 0)
    def _():
        run_ref[...] = jnp.zeros_like(run_ref)

    idx = idx_ref[...]
    tm = idx.shape[0]
    lane = lax.broadcasted_iota(I32, (tm, N_EXPERTS), 1)
    hits = [lane == idx[:, k:k + 1] for k in range(TOP_K)]
    onehot = sum(hit.astype(F32) for hit in hits)
    r = lax.broadcasted_iota(I32, (tm, tm), 0)
    c = lax.broadcasted_iota(I32, (tm, tm), 1)
    lower = (r > c).astype(BF16)
    before = _dot(lower, onehot.astype(BF16)) + run_ref[...]
    ranks = [jnp.sum(jnp.where(hit, before, 0.0), -1, keepdims=True) for hit in hits]
    rank_ref[...] = jnp.concatenate(ranks, 1).astype(I32)
    run_ref[...] = run_ref[...] + jnp.sum(onehot, 0, keepdims=True)
    cnt_ref[...] = run_ref[...]


def _dest_kernel(idx_ref, rank_ref, start_ref, dest_ref):
    idx = idx_ref[...]
    lane = lax.broadcasted_iota(I32, (idx.shape[0], N_EXPERTS), 1)
    starts = start_ref[...]
    cols = [jnp.sum(jnp.where(lane == idx[:, k:k + 1], starts, 0), -1, keepdims=True)
            for k in range(TOP_K)]
    dest_ref[...] = rank_ref[...] + jnp.concatenate(cols, 1)


def _row_copy(src_ref, src_row, dst_ref, dst_row, sem):
    return pltpu.make_async_copy(src_ref.at[pl.ds(src_row, 1), :], dst_ref.at[pl.ds(dst_row, 1), :], sem)


def _index_copy(dest_hbm, dest_smem, sem_idx, step, slot, n):
    return pltpu.make_async_copy(dest_hbm.at[pl.ds(pl.multiple_of(step * n, n), n)],
                                 dest_smem.at[pl.ds(pl.multiple_of(slot * n, n), n)], sem_idx.at[slot])


def _dispatch_kernel(zero_rows, dest_hbm, x_ref, xs_hbm, dest_smem, zero_buf, sem_idx, sem_rows):
    tm = x_ref.shape[0]
    n = tm * TOP_K
    i = pl.program_id(0)
    last = pl.num_programs(0) - 1
    slot = i % 2

    @pl.when(i == 0)
    def _():
        _index_copy(dest_hbm, dest_smem, sem_idx, 0, 0, n).start()
        zero_buf[...] = jnp.zeros_like(zero_buf)
        bm = zero_buf.shape[0]

        def zero_tile(t):
            row = pl.multiple_of(jnp.maximum(zero_rows[t], 0), bm)
            return pltpu.make_async_copy(zero_buf, xs_hbm.at[pl.ds(row, bm), :], sem_rows)

        for t in range(zero_rows.shape[0]):
            pl.when(zero_rows[t] >= 0)(lambda t=t: zero_tile(t).start())
        for t in range(zero_rows.shape[0]):
            pl.when(zero_rows[t] >= 0)(lambda t=t: zero_tile(t).wait())

    _index_copy(dest_hbm, dest_smem, sem_idx, i, slot, n).wait()

    @pl.when(i < last)
    def _():
        _index_copy(dest_hbm, dest_smem, sem_idx, i + 1, 1 - slot, n).start()

    base = slot * n

    def issue(jj, _):
        for u in range(ROW_UNROLL):
            j = jj * ROW_UNROLL + u
            tok = jj * (ROW_UNROLL // TOP_K) + u // TOP_K
            _row_copy(x_ref, tok, xs_hbm, dest_smem[base + j], sem_rows).start(priority=u % 2)
        return 0

    lax.fori_loop(0, n // ROW_UNROLL, issue, 0)
    for _ in range(TOP_K):
        pltpu.make_async_copy(x_ref, xs_hbm.at[pl.ds(0, tm), :], sem_rows).wait()


def _expert_kernel(te_ref, tv_ref, xs_ref, w1_ref, b1_ref, w2_ref, b2_ref, ys_ref):
    i = pl.program_id(0)

    @pl.when(tv_ref[i] != 0)
    def _():
        ff = w2_ref.shape[0]
        h = _dot(xs_ref[...].astype(BF16), w1_ref[...]) + b1_ref[...]
        h_gate = jnp.minimum(h[:, :ff], SWIGLU_LIMIT)
        h_up = jnp.clip(h[:, ff:], -SWIGLU_LIMIT, SWIGLU_LIMIT)
        act = h_gate * _sigmoid(SWIGLU_ALPHA * h_gate) * (h_up + 1.0)
        ys_ref[...] = _dot(act.astype(BF16), w2_ref[...]) + b2_ref[...]

    @pl.when(tv_ref[i] == 0)
    def _():
        ys_ref[...] = jnp.zeros_like(ys_ref)


def _combine_kernel(dest_hbm, ys_hbm, x_ref, gate_ref, g_ref, b_ref, o_ref,
                    dest_smem, buf, sem_idx, sem_rows, *, alpha):
    tm = x_ref.shape[0]
    n = tm * TOP_K
    i = pl.program_id(0)
    last = pl.num_programs(0) - 1
    slot = i % 2

    def index_copy(step):
        return _index_copy(dest_hbm, dest_smem, sem_idx, step, step % 3, n)

    def issue_rows(step):
        s = step % 2
        base = (step % 3) * n

        def issue(jj, _):
            for u in range(ROW_UNROLL):
                j = jj * ROW_UNROLL + u
                tok = jj * (ROW_UNROLL // TOP_K) + u // TOP_K
                _row_copy(ys_hbm, dest_smem[base + j], buf.at[s, u % TOP_K], tok,
                          sem_rows.at[s]).start(priority=u % 2)
            return 0

        lax.fori_loop(0, n // ROW_UNROLL, issue, 0)

    @pl.when(i == 0)
    def _():
        index_copy(0).start()
        index_copy(0).wait()

    @pl.when((i == 0) & (last > 0))
    def _():
        index_copy(1).start()

    @pl.when(i == 0)
    def _():
        issue_rows(0)

    @pl.when(i < last)
    def _():
        index_copy(i + 1).wait()

    @pl.when(i + 1 < last)
    def _():
        index_copy(i + 2).start()

    @pl.when(i < last)
    def _():
        issue_rows(i + 1)

    for k in range(TOP_K):
        pltpu.make_async_copy(ys_hbm.at[pl.ds(0, tm), :], buf.at[slot, k], sem_rows.at[slot]).wait()
    gates = gate_ref[...]
    y = sum(gates[:, k:k + 1] * buf[slot, k] for k in range(TOP_K))
    o_ref[...] = _layer_norm(alpha * x_ref[...] + y, g_ref[...], b_ref[...])


def _moe(x2, idx, gates, w1, b1, w2, b2, ln_g, ln_b, alpha):
    T, d = x2.shape
    ff = w2.shape[1]
    tk = T * TOP_K
    bm = EXPERT_TILE
    n_tiles = -(-tk // bm) + N_EXPERTS
    R = n_tiles * bm

    tr = min(TOKEN_TILE, T)
    small = lambda tm: pl.BlockSpec((tm, TOP_K), lambda i: (i, 0))
    rank, counts = pl.pallas_call(
        _rank_kernel, grid=(T // tr,),
        in_specs=[small(tr)], out_specs=[small(tr), _const_spec((1, N_EXPERTS))],
        out_shape=[jax.ShapeDtypeStruct((T, TOP_K), I32), jax.ShapeDtypeStruct((1, N_EXPERTS), F32)],
        scratch_shapes=[pltpu.VMEM((1, N_EXPERTS), F32)],
        compiler_params=_params("arbitrary"),
    )(idx)

    counts = counts.reshape(N_EXPERTS).astype(I32)
    padded = (counts + bm - 1) // bm * bm
    pad_ends = jnp.cumsum(padded)
    pad_starts = pad_ends - padded
    tile_row = jnp.arange(n_tiles, dtype=I32) * bm
    tile_e = jnp.minimum(jnp.sum(pad_ends[None, :] <= tile_row[:, None], 1), N_EXPERTS - 1).astype(I32)
    tile_valid = (tile_row < pad_ends[-1]).astype(I32)

    dest = pl.pallas_call(
        _dest_kernel, grid=(T // tr,),
        in_specs=[small(tr), small(tr), _const_spec((1, N_EXPERTS))], out_specs=small(tr),
        out_shape=jax.ShapeDtypeStruct((T, TOP_K), I32),
        compiler_params=_params("arbitrary"),
    )(idx, rank, pad_starts.reshape(1, N_EXPERTS)).reshape(tk)

    tm = min(ROUTE_TILE, T)
    n = tm * TOP_K
    any_spec = pl.BlockSpec(memory_space=pl.ANY)
    last_tile = jnp.where(padded > 0, pad_ends - bm, -1)
    tail = pad_ends[-1] + jnp.arange(n_tiles - tk // bm, dtype=I32) * bm
    tail = jnp.where(tail < R, tail, -1)
    zero_rows = jnp.concatenate([last_tile, tail]).astype(I32)
    xs = pl.pallas_call(
        _dispatch_kernel,
        grid_spec=pltpu.PrefetchScalarGridSpec(
            num_scalar_prefetch=1, grid=(T // tm,),
            in_specs=[any_spec, pl.BlockSpec((tm, d), lambda i, zr: (i, 0))], out_specs=any_spec,
            scratch_shapes=[pltpu.SMEM((2 * n,), I32), pltpu.VMEM((bm, d), F32),
                            pltpu.SemaphoreType.DMA((2,)), pltpu.SemaphoreType.DMA(())]),
        out_shape=jax.ShapeDtypeStruct((R, d), F32),
        compiler_params=_params("arbitrary"),
    )(zero_rows, dest, x2)

    ys = pl.pallas_call(
        _expert_kernel,
        grid_spec=pltpu.PrefetchScalarGridSpec(
            num_scalar_prefetch=2, grid=(n_tiles,),
            in_specs=[pl.BlockSpec((bm, d), lambda i, te, tv: (i, 0)),
                      pl.BlockSpec((None, d, 2 * ff), lambda i, te, tv: (te[i], 0, 0)),
                      pl.BlockSpec((None, 1, 2 * ff), lambda i, te, tv: (te[i], 0, 0)),
                      pl.BlockSpec((None, ff, d), lambda i, te, tv: (te[i], 0, 0)),
                      pl.BlockSpec((None, 1, d), lambda i, te, tv: (te[i], 0, 0))],
            out_specs=pl.BlockSpec((bm, d), lambda i, te, tv: (i, 0))),
        out_shape=jax.ShapeDtypeStruct((R, d), F32),
        compiler_params=_params("arbitrary"),
    )(tile_e, tile_valid, xs, w1.astype(BF16), b1.reshape(N_EXPERTS, 1, 2 * ff),
      w2.astype(BF16), b2.reshape(N_EXPERTS, 1, d))

    return pl.pallas_call(
        functools.partial(_combine_kernel, alpha=alpha), grid=(T // tm,),
        in_specs=[any_spec, any_spec, _token_spec(tm, d), small(tm), _const_spec((1, d)), _const_spec((1, d))],
        out_specs=_token_spec(tm, d),
        out_shape=jax.ShapeDtypeStruct((T, d), F32),
        scratch_shapes=[pltpu.SMEM((3 * n,), I32), pltpu.VMEM((2, TOP_K, tm, d), F32),
                        pltpu.SemaphoreType.DMA((3,)), pltpu.SemaphoreType.DMA((2,))],
        compiler_params=_params("arbitrary"),
    )(dest, ys, x2, gates, ln_g.reshape(1, d), ln_b.reshape(1, d))


def kernel(x, mem, positions, w_in, diff_lambda_q1, diff_lambda_k1, diff_lambda_q2, diff_lambda_k2, diff_subln_g, w_diff_o, gdn_conv_w, gdn_A_log, gdn_dt_bias, gdn_norm_g, w_gdn_o, w_mix_o, ln1_g, ln1_b, w_cq, w_ck, w_cv, w_co, ln2_g, ln2_b, w_router, b_router, w_exp_in, b_exp_in, w_exp_out, b_exp_out, ln3_g, ln3_b):
    B, S, d = x.shape
    depth = w_in.shape[0]
    alpha = (2 * depth) ** 0.25
    rope = _rope_tables(positions)
    xf = x.reshape(B * S, d)
    for l in range(depth):
        lambda_init = 0.8 - 0.6 * math.exp(-0.3 * l)
        dq, dk, dvt, (gq, gk, gv), sz, sga, sgb, gdn_col = _input_projections(
            xf, w_in[l], rope, gdn_conv_w[l], gdn_A_log[l], gdn_dt_bias[l], B, S)
        lam_vecs = jnp.stack([diff_lambda_q1[l], diff_lambda_k1[l], diff_lambda_q2[l], diff_lambda_k2[l]])
        od = _diff_attention(dq, dk, dvt, lam_vecs, diff_subln_g[l], B, S, lambda_init)
        og = _gated_deltanet(gq, gk, gv, sz, gdn_col, gdn_norm_g[l], B, S)
        x1 = _mix(xf, od, og, sga, sgb, w_diff_o[l], w_gdn_o[l], w_mix_o[l], ln1_g[l], ln1_b[l], alpha)
        x2, idx, gates = _cross_attention(x1, mem, w_cq[l], w_ck[l], w_cv[l], w_co[l], ln2_g[l], ln2_b[l],
                                          w_router[l], b_router[l], B, S, alpha)
        xf = _moe(x2, idx, gates, w_exp_in[l], b_exp_in[l], w_exp_out[l], b_exp_out[l],
                  ln3_g[l], ln3_b[l], alpha)
    return xf.reshape(B, S, d)
```
